```python
import jax, jax.numpy as jnp
from jax import lax
import numpy as np

D_MODEL = 4096
BATCH = 4
SEQ = 2048
DEPTH = 1
DEC_BATCH = 32
DEC_SEQ = 4
PAST_LEN = 8192
PAGE_SIZE = 128

D_CONV = D_MODEL // 2
CONV_WIDTH = 31
HEAD_DIM = 128
N_HEADS = (D_MODEL // 2) // HEAD_DIM
D_ATTN = N_HEADS * HEAD_DIM
MOBA_BLOCK = 256
MOBA_TOPK = 3
Q_CHUNK = 16
N_EXPERTS = 32
TOP_K = 4
D_FF = D_MODEL
SWIGLU_ALPHA = 1.702
SWIGLU_LIMIT = 7.0
EXPERT_ROWS = 128
D_IN = 2 * D_CONV + 3 * D_ATTN + 2 * D_MODEL
EPS = 1e-6
NEG = -1e30

kernel_name = "hybrid_conformer_moba_moe_step"


def rms_norm(x, g):
    xf = x.astype(jnp.float32)
    y = xf * lax.rsqrt(jnp.mean(xf * xf, axis=-1, keepdims=True) + EPS)
    return (y * g.astype(jnp.float32)).astype(x.dtype)


def layer_norm(x, g, b):
    xf = x.astype(jnp.float32)
    mu = jnp.mean(xf, axis=-1, keepdims=True)
    var = jnp.mean(jnp.square(xf - mu), axis=-1, keepdims=True)
    y = (xf - mu) * lax.rsqrt(var + EPS)
    return (y * g.astype(jnp.float32) + b.astype(jnp.float32)).astype(x.dtype)


def split_in(h):
    cuts = [int(c) for c in np.cumsum([2 * D_CONV, D_ATTN, D_ATTN, D_ATTN, D_MODEL])]
    return jnp.split(h, cuts, axis=-1)


def glu(u):
    a, b = jnp.split(u, 2, axis=-1)
    return a * jax.nn.sigmoid(b)


def heads(t):
    return t.reshape(t.shape[0], t.shape[1], N_HEADS, HEAD_DIM)


def conv_branch(u_ctx, conv_w, conv_b, ln_g, ln_b, w_conv_out):
    y = lax.conv_general_dilated(
        u_ctx, conv_w[:, None, :].astype(u_ctx.dtype), window_strides=(1,), padding='VALID',
        dimension_numbers=('NWC', 'WIO', 'NWC'), feature_group_count=D_CONV)
    y = layer_norm(y + conv_b, ln_g, ln_b)
    return jax.nn.silu(y) @ w_conv_out


def gated_merge(yc, ya, gc, ga, w_o):
    return (jax.nn.sigmoid(gc) * yc + jax.nn.sigmoid(ga) * ya) @ w_o


def moba_softmax(q, k_sel, v_sel, sel_valid, k_own, v_own, own_mask):
    scale = HEAD_DIM ** -0.5
    s_sel = jnp.einsum('bhqd,bhqkd->bhqk', q, k_sel).astype(jnp.float32) * scale
    s_own = jnp.einsum('bhqd,bhkd->bhqk', q, k_own).astype(jnp.float32) * scale
    s = jnp.concatenate([jnp.where(sel_valid, s_sel, NEG), jnp.where(own_mask, s_own, NEG)], axis=-1)
    p = jax.nn.softmax(s, axis=-1).astype(v_sel.dtype)
    n_s = k_sel.shape[3]
    return (jnp.einsum('bhqk,bhqkd->bhqd', p[..., :n_s], v_sel)
            + jnp.einsum('bhqk,bhkd->bhqd', p[..., n_s:], v_own))


def prompt_attention(q, k, v):
    B, S = q.shape[0], q.shape[1]
    nb = -(-S // MOBA_BLOCK)
    s_pad = nb * MOBA_BLOCK
    n_sel = min(MOBA_TOPK, nb)
    qh = q.transpose(0, 2, 1, 3)
    pad = ((0, 0), (0, s_pad - S), (0, 0), (0, 0))
    kb = jnp.pad(k, pad).transpose(0, 2, 1, 3).reshape(B, N_HEADS, nb, MOBA_BLOCK, HEAD_DIM)
    vb = jnp.pad(v, pad).transpose(0, 2, 1, 3).reshape(B, N_HEADS, nb, MOBA_BLOCK, HEAD_DIM)
    k_means = jnp.mean(kb.astype(jnp.float32), axis=3)
    b_idx = jnp.arange(B)[:, None, None, None]
    h_idx = jnp.arange(N_HEADS)[None, :, None, None]

    def chunk(c):
        start = c * Q_CHUNK
        j = start // MOBA_BLOCK
        q_c = lax.dynamic_slice_in_dim(qh, start, Q_CHUNK, axis=2)
        sc = jnp.einsum('bhqd,bhnd->bhqn', q_c.astype(jnp.float32), k_means)
        sc = jnp.where(jnp.arange(nb) < j, sc, NEG)
        _, sel = lax.top_k(sc, n_sel)
        k_sel = kb[b_idx, h_idx, sel].reshape(B, N_HEADS, Q_CHUNK, n_sel * MOBA_BLOCK, HEAD_DIM)
        v_sel = vb[b_idx, h_idx, sel].reshape(B, N_HEADS, Q_CHUNK, n_sel * MOBA_BLOCK, HEAD_DIM)
        sel_valid = jnp.repeat(sel < j, MOBA_BLOCK, axis=-1)
        k_own = lax.dynamic_index_in_dim(kb, j, axis=2, keepdims=False)
        v_own = lax.dynamic_index_in_dim(vb, j, axis=2, keepdims=False)
        own_mask = (j * MOBA_BLOCK + jnp.arange(MOBA_BLOCK))[None, :] <= (start + jnp.arange(Q_CHUNK))[:, None]
        return moba_softmax(q_c, k_sel, v_sel, sel_valid, k_own, v_own, own_mask)

    out = lax.map(chunk, jnp.arange(S // Q_CHUNK))
    return out.transpose(1, 0, 3, 2, 4).reshape(B, S, D_ATTN)


def sample_attention(q, k, v, cache_k, cache_v, page_table):
    DB, DS = q.shape[0], q.shape[1]
    n_pages = PAST_LEN // PAGE_SIZE
    ppb = MOBA_BLOCK // PAGE_SIZE
    J = PAST_LEN // MOBA_BLOCK
    nbm = max(J, 1)
    n_sel = min(MOBA_TOPK, nbm)
    k_past = cache_k[page_table].reshape(DB, PAST_LEN, N_HEADS, HEAD_DIM)
    rows_m = nbm * MOBA_BLOCK
    k_m = jnp.pad(k_past, ((0, 0), (0, max(0, rows_m - PAST_LEN)), (0, 0), (0, 0)))[:, :rows_m]
    k_means = jnp.mean(k_m.astype(jnp.float32).reshape(DB, nbm, MOBA_BLOCK, N_HEADS, HEAD_DIM), axis=2)
    k_means = k_means.transpose(0, 2, 1, 3)
    qh = q.transpose(0, 2, 1, 3)
    sc = jnp.einsum('bhqd,bhnd->bhqn', qh.astype(jnp.float32), k_means)
    sc = jnp.where(jnp.arange(nbm) < J, sc, NEG)
    _, sel = lax.top_k(sc, n_sel)
    lp = jnp.minimum(sel[..., None] * ppb + jnp.arange(ppb), n_pages - 1)
    phys = page_table[jnp.arange(DB)[:, None, None, None, None], lp]
    h_idx = jnp.arange(N_HEADS)[None, :, None, None, None, None]
    tok = jnp.arange(PAGE_SIZE)
    k_sel = cache_k[phys[..., None], tok, h_idx].reshape(DB, N_HEADS, DS, n_sel * MOBA_BLOCK, HEAD_DIM)
    v_sel = cache_v[phys[..., None], tok, h_idx].reshape(DB, N_HEADS, DS, n_sel * MOBA_BLOCK, HEAD_DIM)
    sel_valid = jnp.repeat(sel < J, MOBA_BLOCK, axis=-1)
    R = PAST_LEN - J * MOBA_BLOCK
    v_own_past = cache_v[page_table[:, J * ppb:]].reshape(DB, R, N_HEADS, HEAD_DIM)
    k_own = jnp.concatenate([k_past[:, J * MOBA_BLOCK:], k], axis=1).transpose(0, 2, 1, 3)
    v_own = jnp.concatenate([v_own_past, v], axis=1).transpose(0, 2, 1, 3)
    own_mask = jnp.arange(R + DS)[None, :] <= (R + jnp.arange(DS))[:, None]
    out = moba_softmax(qh, k_sel, v_sel, sel_valid, k_own, v_own, own_mask)
    return out.transpose(0, 2, 1, 3).reshape(DB, DS, D_ATTN)


def moe_ffn(xn, w_router, b_router, w_up, b_up, w_down, b_down):
    T = xn.shape[0]
    logits = xn.astype(jnp.float32) @ w_router.astype(jnp.float32) + b_router.astype(jnp.float32)
    top_val, top_idx = lax.top_k(logits, TOP_K)
    gates = jax.nn.softmax(top_val, axis=-1)
    tk = T * TOP_K
    flat_e = top_idx.reshape(-1)
    flat_tok = jnp.repeat(jnp.arange(T, dtype=jnp.int32), TOP_K)
    order = jnp.argsort(flat_e, stable=True)
    se, st, sg = flat_e[order], flat_tok[order], gates.reshape(-1)[order]
    counts = jax.ops.segment_sum(jnp.ones((tk,), jnp.int32), flat_e, num_segments=N_EXPERTS)
    padded = (counts + EXPERT_ROWS - 1) // EXPERT_ROWS * EXPERT_ROWS
    starts = jnp.cumsum(counts) - counts
    pends = jnp.cumsum(padded)
    pstarts = pends - padded
    dest = pstarts[se] + (jnp.arange(tk, dtype=jnp.int32) - starts[se])
    n_blocks = (tk + N_EXPERTS * (EXPERT_ROWS - 1) + EXPERT_ROWS - 1) // EXPERT_ROWS
    n_rows = n_blocks * EXPERT_ROWS
    row_tok = jnp.full((n_rows,), T, jnp.int32).at[dest].set(st)
    row_g = jnp.zeros((n_rows,), jnp.float32).at[dest].set(sg)
    block_e = jnp.minimum(jnp.searchsorted(pends, jnp.arange(n_blocks) * EXPERT_ROWS, side='right'), N_EXPERTS - 1)
    x_pad = jnp.concatenate([xn, jnp.zeros((1, D_MODEL), xn.dtype)], axis=0)
    xb = x_pad[row_tok].reshape(n_blocks, EXPERT_ROWS, D_MODEL)

    def expert_block(args):
        xblk, e = args
        h = xblk @ w_up[e] + b_up[e]
        x_glu = jnp.minimum(h[:, :D_FF], SWIGLU_LIMIT)
        x_lin = jnp.clip(h[:, D_FF:], -SWIGLU_LIMIT, SWIGLU_LIMIT)
        act = x_glu * jax.nn.sigmoid(SWIGLU_ALPHA * x_glu) * (x_lin + 1)
        return act @ w_down[e] + b_down[e]

    yb = lax.map(expert_block, (xb, block_e)).reshape(n_rows, D_MODEL)
    y = jax.ops.segment_sum(yb * row_g[:, None].astype(yb.dtype), row_tok, num_segments=T + 1)
    return y[:T]


def setup_inputs(seed: int = 0) -> dict:
    key = jax.random.key(seed)
    ks = jax.random.split(key, 24)
    f32 = jnp.float32
    n_pages = PAST_LEN // PAGE_SIZE
    n_used = DEC_BATCH * n_pages
    n_phys = n_used + max(1, n_used // 4)
    nrm = lambda k, shp, s: jax.random.normal(k, shp, f32) * s
    page_table = jax.random.permutation(ks[5], n_phys)[:n_used].reshape(DEC_BATCH, n_pages).astype(jnp.int32)
    return {
        "x_prompt": nrm(ks[0], (BATCH, SEQ, D_MODEL), 1.0),
        "x_sample": nrm(ks[1], (DEC_BATCH, DEC_SEQ, D_MODEL), 1.0),
        "cache_k": nrm(ks[2], (n_phys, PAGE_SIZE, N_HEADS, HEAD_DIM), 1.0),
        "cache_v": nrm(ks[3], (n_phys, PAGE_SIZE, N_HEADS, HEAD_DIM), 1.0),
        "state_conv": nrm(ks[4], (DEC_BATCH, CONV_WIDTH - 1, D_CONV), 0.5),
        "page_table": page_table,
        "attn_norm_g": 1.0 + nrm(ks[6], (D_MODEL,), 0.02),
        "w_in": nrm(ks[7], (D_MODEL, D_IN), D_MODEL ** -0.5),
        "conv_w": nrm(ks[8], (CONV_WIDTH, D_CONV), CONV_WIDTH ** -0.5),
        "conv_b": nrm(ks[9], (D_CONV,), 0.01),
        "conv_ln_g": 1.0 + nrm(ks[10], (D_CONV,), 0.02),
        "conv_ln_b": nrm(ks[11], (D_CONV,), 0.01),
        "w_conv_out": nrm(ks[12], (D_CONV, D_MODEL), D_CONV ** -0.5),
        "q_norm_g": 1.0 + nrm(ks[13], (HEAD_DIM,), 0.02),
        "k_norm_g": 1.0 + nrm(ks[14], (HEAD_DIM,), 0.02),
        "w_attn_out": nrm(ks[15], (D_ATTN, D_MODEL), D_ATTN ** -0.5),
        "w_o": nrm(ks[16], (D_MODEL, D_MODEL), D_MODEL ** -0.5),
        "ffn_norm_g": 1.0 + nrm(ks[17], (D_MODEL,), 0.02),
        "w_router": nrm(ks[18], (D_MODEL, N_EXPERTS), D_MODEL ** -0.5),
        "b_router": nrm(ks[19], (N_EXPERTS,), 0.01),
        "w_up": nrm(ks[20], (N_EXPERTS, D_MODEL, 2 * D_FF), D_MODEL ** -0.5),
        "b_up": nrm(ks[21], (N_EXPERTS, 2 * D_FF), 0.01),
        "w_down": nrm(ks[22], (N_EXPERTS, D_FF, D_MODEL), D_FF ** -0.5),
        "b_down": nrm(ks[23], (N_EXPERTS, D_MODEL), 0.01),
    }


def reference(x_prompt, x_sample, cache_k, cache_v, state_conv, page_table, attn_norm_g, w_in,
              conv_w, conv_b, conv_ln_g, conv_ln_b, w_conv_out, q_norm_g, k_norm_g, w_attn_out,
              w_o, ffn_norm_g, w_router, b_router, w_up, b_up, w_down, b_down):
    y_prompt, y_sample = x_prompt, x_sample
    for _ in range(DEPTH):
        u, q, k, v, gc, ga = split_in(rms_norm(y_prompt, attn_norm_g) @ w_in)
        u = glu(u)
        u_ctx = jnp.pad(u, ((0, 0), (CONV_WIDTH - 1, 0), (0, 0)))
        conv_prompt = u_ctx[:, -(CONV_WIDTH - 1):]
        yc = conv_branch(u_ctx, conv_w, conv_b, conv_ln_g, conv_ln_b, w_conv_out)
        qn = rms_norm(heads(q), q_norm_g)
        k_prompt = rms_norm(heads(k), k_norm_g)
        v_prompt = heads(v)
        ya = prompt_attention(qn, k_prompt, v_prompt) @ w_attn_out
        y_prompt = y_prompt + gated_merge(yc, ya, gc, ga, w_o)
        hn = rms_norm(y_prompt, ffn_norm_g).reshape(-1, D_MODEL)
        y_prompt = y_prompt + moe_ffn(hn, w_router, b_router, w_up, b_up, w_down, b_down).reshape(y_prompt.shape)

        u, q, k, v, gc, ga = split_in(rms_norm(y_sample, attn_norm_g) @ w_in)
        u = glu(u)
        u_ctx = jnp.concatenate([state_conv.astype(u.dtype), u], axis=1)
        conv_sample = u_ctx[:, -(CONV_WIDTH - 1):]
        yc = conv_branch(u_ctx, conv_w, conv_b, conv_ln_g, conv_ln_b, w_conv_out)
        qn = rms_norm(heads(q), q_norm_g)
        k_sample = rms_norm(heads(k), k_norm_g)
        v_sample = heads(v)
        ya = sample_attention(qn, k_sample, v_sample, cache_k, cache_v, page_table) @ w_attn_out
        y_sample = y_sample + gated_merge(yc, ya, gc, ga, w_o)
        hn = rms_norm(y_sample, ffn_norm_g).reshape(-1, D_MODEL)
        y_sample = y_sample + moe_ffn(hn, w_router, b_router, w_up, b_up, w_down, b_down).reshape(y_sample.shape)
    return (y_prompt, y_sample, k_prompt, v_prompt, conv_prompt, k_sample, v_sample, conv_sample)
```

```python
import functools

import jax
import jax.numpy as jnp
from jax import lax
from jax.experimental import pallas as pl
from jax.experimental.pallas import tpu as pltpu

F32 = jnp.float32
BF16 = jnp.bfloat16
I32 = jnp.int32

EPS = 1e-6
NEG = -1e30
HEAD_DIM = 128
MOBA_BLOCK = 256
MOBA_TOPK = 3
PAGE_SIZE = 128
CONV_WIDTH = 31
N_EXPERTS = 32
TOP_K = 4
SWIGLU_ALPHA = 1.702
SWIGLU_LIMIT = 7.0

LANES = 128
CONV_HALO = 32
V7X_VMEM_LIMIT = 56 * 1024 * 1024

_NT = (((1,), (1,)), ((), ()))
_TN = (((0,), (0,)), ((), ()))


def _params(*sem):
    return pltpu.CompilerParams(dimension_semantics=sem, vmem_limit_bytes=V7X_VMEM_LIMIT)


def _cast_rows(src_ref, dst_ref, chunk):
    rows = src_ref.shape[0]
    chunk = min(chunk, rows)

    def body(c, carry):
        r = pl.multiple_of(c * chunk, chunk)
        dst_ref[pl.ds(r, chunk), :] = src_ref[pl.ds(r, chunk), :].astype(dst_ref.dtype)
        return carry

    lax.fori_loop(0, rows // chunk, body, 0)


def _rmsnorm_kernel(x_ref, g_ref, o_ref):
    x = x_ref[...]
    y = x * lax.rsqrt(jnp.mean(x * x, axis=-1, keepdims=True) + EPS)
    o_ref[...] = (y * g_ref[...]).astype(o_ref.dtype)


def rmsnorm_rows(x, g, tm, out_dtype):
    t, d = x.shape
    return pl.pallas_call(
        _rmsnorm_kernel,
        grid=(t // tm,),
        in_specs=[pl.BlockSpec((tm, d), lambda i: (i, 0)), pl.BlockSpec((1, d), lambda i: (0, 0))],
        out_specs=pl.BlockSpec((tm, d), lambda i: (i, 0)),
        out_shape=jax.ShapeDtypeStruct((t, d), out_dtype),
        compiler_params=_params("arbitrary"),
        name="rmsnorm_rows",
    )(x, g.reshape(1, d))


def _proj_kernel(x_ref, *rest, n_w, kind, tn):
    w_refs, rest = rest[:n_w], rest[n_w:]
    if kind in ("headnorm", "residual"):
        e_ref, rest = rest[0], rest[1:]
    o_ref, wb_refs = rest[0], rest[1:]

    @pl.when(pl.program_id(1) == 0)
    def _():
        for w_ref, wb_ref in zip(w_refs, wb_refs):
            _cast_rows(w_ref, wb_ref, 256)

    x = x_ref[...]
    acc = jnp.dot(x, wb_refs[0][...], preferred_element_type=F32)
    if kind == "glu":
        gate = jnp.dot(x, wb_refs[1][...], preferred_element_type=F32)
        o_ref[...] = (acc * jax.nn.sigmoid(gate)).astype(o_ref.dtype)
    elif kind == "headnorm":
        g = e_ref[...]
        for h in range(tn // HEAD_DIM):
            sl = slice(h * HEAD_DIM, (h + 1) * HEAD_DIM)
            blk = acc[:, sl]
            inv = lax.rsqrt(jnp.mean(blk * blk, axis=-1, keepdims=True) + EPS)
            o_ref[:, sl] = (blk * inv * g).astype(o_ref.dtype)
    elif kind == "sigmoid":
        o_ref[...] = jax.nn.sigmoid(acc).astype(o_ref.dtype)
    elif kind == "residual":
        o_ref[...] = (e_ref[...] + acc).astype(o_ref.dtype)
    else:
        o_ref[...] = acc.astype(o_ref.dtype)


def dense_proj(x, w, col0, width, *, kind, out_dtype, tm, tn, extra=None):
    t, k = x.shape
    n_w = 2 if kind == "glu" else 1
    c0 = col0 // tn
    in_specs = [pl.BlockSpec((tm, k), lambda j, i: (i, 0)),
                pl.BlockSpec((k, tn), lambda j, i: (0, c0 + j))]
    args = [x, w]
    if kind == "glu":
        c1 = (col0 + width) // tn
        in_specs.append(pl.BlockSpec((k, tn), lambda j, i: (0, c1 + j)))
        args.append(w)
    if kind == "headnorm":
        in_specs.append(pl.BlockSpec((1, HEAD_DIM), lambda j, i: (0, 0)))
        args.append(extra.reshape(1, HEAD_DIM))
    if kind == "residual":
        in_specs.append(pl.BlockSpec((tm, tn), lambda j, i: (i, j)))
        args.append(extra)
    return pl.pallas_call(
        functools.partial(_proj_kernel, n_w=n_w, kind=kind, tn=tn),
        grid=(width // tn, t // tm),
        in_specs=in_specs,
        out_specs=pl.BlockSpec((tm, tn), lambda j, i: (i, j)),
        out_shape=jax.ShapeDtypeStruct((t, width), out_dtype),
        scratch_shapes=[pltpu.VMEM((k, tn), BF16) for _ in range(n_w)],
        compiler_params=_params("arbitrary", "arbitrary"),
        name="proj_" + kind,
    )(*args)


def _conv_kernel(halo_ref, u_ref, cw_ref, cb_ref, lg_ref, lb_ref, o_ref, ctx_ref, y_ref, *, tt, zero_first_halo,
                 rc, cc):
    if zero_first_halo:
        first = pl.program_id(1) == 0

        @pl.when(first)
        def _():
            ctx_ref[0:CONV_HALO, :] = jnp.zeros((CONV_HALO, ctx_ref.shape[1]), F32)

        @pl.when(jnp.logical_not(first))
        def _():
            ctx_ref[0:CONV_HALO, :] = halo_ref[...]
    else:
        ctx_ref[0:CONV_HALO, :] = halo_ref[...]
    ctx_ref[CONV_HALO:CONV_HALO + tt, :] = u_ref[...]
    d = ctx_ref.shape[1]
    lead = CONV_HALO - (CONV_WIDTH - 1)
    sub = 8

    for c0 in range(0, d, cc):
        def conv_rows(c, carry, c0=c0):
            r = pl.multiple_of(c * rc, rc)
            window = ctx_ref[pl.ds(r, rc + CONV_HALO), c0:c0 + cc]
            acc = jnp.zeros((rc, cc), F32)
            for b in range(sub):
                taps = [s for s in range(lead, lead + CONV_WIDTH) if s % sub == b]
                span = taps[-1] - b + rc
                shifted = window[b:b + span, :]
                for s in taps:
                    w = s - lead
                    acc = acc + shifted[s - b:s - b + rc, :] * cw_ref[w:w + 1, c0:c0 + cc]
            y_ref[pl.ds(r, rc), c0:c0 + cc] = acc + cb_ref[:, c0:c0 + cc]
            return carry

        lax.fori_loop(0, tt // rc, conv_rows, 0)

    lg = lg_ref[...]
    lb = lb_ref[...]

    def norm_rows(c, carry):
        r = pl.multiple_of(c * rc, rc)
        y = y_ref[pl.ds(r, rc), :]
        mu = jnp.mean(y, axis=-1, keepdims=True)
        yc = y - mu
        var = jnp.mean(yc * yc, axis=-1, keepdims=True)
        z = yc * lax.rsqrt(var + EPS) * lg + lb
        o_ref[pl.ds(r, rc), :] = (z * jax.nn.sigmoid(z)).astype(o_ref.dtype)
        return carry

    lax.fori_loop(0, tt // rc, norm_rows, 0)


def conv_branch_act(u, halo, conv_w, conv_b, ln_g, ln_b, *, n_seq, seq_len, tt, out_dtype):
    d = u.shape[1]
    steps = seq_len // tt
    zero_first = halo is None
    if zero_first:
        hb = tt // CONV_HALO
        halo_arr = u
        halo_spec = pl.BlockSpec((CONV_HALO, d), lambda b, t: (jnp.maximum((b * steps + t) * hb - 1, 0), 0))
    else:
        halo_arr = halo
        halo_spec = pl.BlockSpec((CONV_HALO, d), lambda b, t: (b, 0))
    row = lambda b, t: (0, 0)
    return pl.pallas_call(
        functools.partial(_conv_kernel, tt=tt, zero_first_halo=zero_first, rc=16, cc=min(256, d)),
        grid=(n_seq, steps),
        in_specs=[halo_spec,
                  pl.BlockSpec((tt, d), lambda b, t: (b * steps + t, 0)),
                  pl.BlockSpec((CONV_WIDTH, d), row),
                  pl.BlockSpec((1, d), row), pl.BlockSpec((1, d), row), pl.BlockSpec((1, d), row)],
        out_specs=pl.BlockSpec((tt, d), lambda b, t: (b * steps + t, 0)),
        out_shape=jax.ShapeDtypeStruct((n_seq * seq_len, d), out_dtype),
        scratch_shapes=[pltpu.VMEM((CONV_HALO + tt, d), F32), pltpu.VMEM((tt, d), F32)],
        compiler_params=_params("arbitrary", "arbitrary"),
        name="conv_ln_silu",
    )(halo_arr, u, conv_w, conv_b.reshape(1, d), ln_g.reshape(1, d), ln_b.reshape(1, d))


def _moba_prompt_kernel(q_ref, k_ref, v_ref, o_ref, kb_ref, vb_ref, km_ref, *, nb):
    blk = MOBA_BLOCK
    j = pl.program_id(2)

    @pl.when(j == 0)
    def _():
        for n in range(nb):
            kf = k_ref[n * blk:(n + 1) * blk, :]
            kb_ref[n * blk:(n + 1) * blk, :] = kf.astype(BF16)
            vb_ref[n * blk:(n + 1) * blk, :] = v_ref[n * blk:(n + 1) * blk, :].astype(BF16)
            km_ref[n:n + 1, :] = jnp.mean(kf, axis=0, keepdims=True)

    qf = q_ref[...]
    sc = lax.dot_general(km_ref[...], qf, _NT, precision=lax.Precision.HIGHEST, preferred_element_type=F32)
    bi = lax.broadcasted_iota(I32, (nb, blk), 0)
    rank = jnp.zeros((nb, blk), I32)
    for m in range(nb):
        row = sc[m:m + 1, :]
        beats = (row > sc) | ((row == sc) & (m < bi))
        rank = rank + jnp.where(beats, (m < j).astype(I32), 0)
    sel = jnp.where((bi < j) & (rank < MOBA_TOPK), 1.0, 0.0)

    qb = qf.astype(BF16)
    ki = lax.broadcasted_iota(I32, (blk, blk), 0)
    qi = lax.broadcasted_iota(I32, (blk, blk), 1)
    causal = jnp.where(ki <= qi, 1.0, 0.0)
    scale = HEAD_DIM ** -0.5

    def body(n, carry):
        m_run, l_run, acc = carry
        r = pl.multiple_of(n * blk, blk)
        s = lax.dot_general(kb_ref[pl.ds(r, blk), :], qb, _NT, preferred_element_type=F32) * scale
        selrow = jnp.max(jnp.where(bi == n, sel, 0.0), axis=0, keepdims=True)
        own = (n == j).astype(F32)
        mask = (own * causal + (1.0 - own) * selrow) > 0.5
        s = jnp.where(mask, s, NEG)
        m_new = jnp.maximum(m_run, jnp.max(s, axis=0, keepdims=True))
        p = jnp.where(mask, jnp.exp(s - m_new), 0.0)
        alpha = jnp.exp(m_run - m_new)
        l_new = alpha * l_run + jnp.sum(p, axis=0, keepdims=True)
        pv = lax.dot_general(vb_ref[pl.ds(r, blk), :], p.astype(BF16), _TN, preferred_element_type=F32)
        return m_new, l_new, alpha * acc + pv

    init = (jnp.full((1, blk), NEG, F32), jnp.zeros((1, blk), F32), jnp.zeros((HEAD_DIM, blk), F32))
    _, l_run, acc = lax.fori_loop(0, j + 1, body, init)
    o_ref[...] = (acc / l_run).T.astype(o_ref.dtype)


def moba_prompt_attention(q, k, v, *, n_batch, seq_len, out_dtype):
    assert seq_len % MOBA_BLOCK == 0
    nb = seq_len // MOBA_BLOCK
    n_heads = q.shape[1] // HEAD_DIM
    qo_spec = pl.BlockSpec((MOBA_BLOCK, HEAD_DIM), lambda b, h, j: (b * nb + j, h))
    kv_spec = pl.BlockSpec((seq_len, HEAD_DIM), lambda b, h, j: (b, h))
    return pl.pallas_call(
        functools.partial(_moba_prompt_kernel, nb=nb),
        grid=(n_batch, n_heads, nb),
        in_specs=[qo_spec, kv_spec, kv_spec],
        out_specs=qo_spec,
        out_shape=jax.ShapeDtypeStruct((n_batch * seq_len, n_heads * HEAD_DIM), out_dtype),
        scratch_shapes=[pltpu.VMEM((seq_len, HEAD_DIM), BF16), pltpu.VMEM((seq_len, HEAD_DIM), BF16),
                        pltpu.VMEM((nb, HEAD_DIM), F32)],
        compiler_params=_params("arbitrary", "arbitrary", "arbitrary"),
        name="moba_prompt",
    )(q, k, v)


def _sample_attention_jax(q, k, v, cache_k, cache_v, page_table, past_len):
    db, ds, n_heads, _ = q.shape
    n_pages = past_len // PAGE_SIZE
    ppb = MOBA_BLOCK // PAGE_SIZE
    jb = past_len // MOBA_BLOCK
    nbm = max(jb, 1)
    n_sel = min(MOBA_TOPK, nbm)
    k_past = cache_k[page_table].reshape(db, past_len, n_heads, HEAD_DIM)
    rows_m = nbm * MOBA_BLOCK
    k_m = jnp.pad(k_past, ((0, 0), (0, max(0, rows_m - past_len)), (0, 0), (0, 0)))[:, :rows_m]
    k_means = jnp.mean(k_m.reshape(db, nbm, MOBA_BLOCK, n_heads, HEAD_DIM), axis=2).transpose(0, 2, 1, 3)
    qh = q.transpose(0, 2, 1, 3)
    sc = jnp.einsum('bhqd,bhnd->bhqn', qh, k_means, precision=lax.Precision.HIGHEST)
    sc = jnp.where(jnp.arange(nbm) < jb, sc, NEG)
    _, sel = lax.top_k(sc, n_sel)
    lp = jnp.minimum(sel[..., None] * ppb + jnp.arange(ppb), n_pages - 1)
    phys = page_table[jnp.arange(db)[:, None, None, None, None], lp]
    h_idx = jnp.arange(n_heads)[None, :, None, None, None, None]
    tok = jnp.arange(PAGE_SIZE)
    k_sel = cache_k[phys[..., None], tok, h_idx].reshape(db, n_heads, ds, n_sel * MOBA_BLOCK, HEAD_DIM)
    v_sel = cache_v[phys[..., None], tok, h_idx].reshape(db, n_heads, ds, n_sel * MOBA_BLOCK, HEAD_DIM)
    sel_valid = jnp.repeat(sel < jb, MOBA_BLOCK, axis=-1)
    r_own = past_len - jb * MOBA_BLOCK
    v_own_past = cache_v[page_table[:, jb * ppb:]].reshape(db, r_own, n_heads, HEAD_DIM)
    k_own = jnp.concatenate([k_past[:, jb * MOBA_BLOCK:], k], axis=1).transpose(0, 2, 1, 3)
    v_own = jnp.concatenate([v_own_past, v], axis=1).transpose(0, 2, 1, 3)
    own_mask = jnp.arange(r_own + ds)[None, :] <= (r_own + jnp.arange(ds))[:, None]
    scale = HEAD_DIM ** -0.5
    s_sel = jnp.einsum('bhqd,bhqkd->bhqk', qh, k_sel) * scale
    s_own = jnp.einsum('bhqd,bhkd->bhqk', qh, k_own) * scale
    s = jnp.concatenate([jnp.where(sel_valid, s_sel, NEG), jnp.where(own_mask, s_own, NEG)], axis=-1)
    p = jax.nn.softmax(s, axis=-1)
    n_s = k_sel.shape[3]
    out = (jnp.einsum('bhqk,bhqkd->bhqd', p[..., :n_s], v_sel) + jnp.einsum('bhqk,bhkd->bhqd', p[..., n_s:], v_own))
    return out.transpose(0, 2, 1, 3).reshape(db * ds, n_heads * HEAD_DIM)


def _merge_kernel(c_ref, a_ref, gc_ref, ga_ref, wc_ref, wa_ref, o_ref, wcb_ref, wab_ref):
    @pl.when(pl.program_id(1) == 0)
    def _():
        _cast_rows(wc_ref, wcb_ref, 256)
        _cast_rows(wa_ref, wab_ref, 256)

    yc = jnp.dot(c_ref[...], wcb_ref[...], preferred_element_type=F32)
    ya = jnp.dot(a_ref[...], wab_ref[...], preferred_element_type=F32)
    o_ref[...] = (gc_ref[...].astype(F32) * yc + ga_ref[...].astype(F32) * ya).astype(o_ref.dtype)


def gated_branch_merge(c_act, attn, sg, w_conv_out, w_attn_out, *, tm, tn, out_dtype):
    t, kc = c_act.shape
    ka = attn.shape[1]
    n = w_conv_out.shape[1]
    nj = n // tn
    return pl.pallas_call(
        _merge_kernel,
        grid=(nj, t // tm),
        in_specs=[pl.BlockSpec((tm, kc), lambda j, i: (i, 0)),
                  pl.BlockSpec((tm, ka), lambda j, i: (i, 0)),
                  pl.BlockSpec((tm, tn), lambda j, i: (i, j)),
                  pl.BlockSpec((tm, tn), lambda j, i: (i, nj + j)),
                  pl.BlockSpec((kc, tn), lambda j, i: (0, j)),
                  pl.BlockSpec((ka, tn), lambda j, i: (0, j))],
        out_specs=pl.BlockSpec((tm, tn), lambda j, i: (i, j)),
        out_shape=jax.ShapeDtypeStruct((t, n), out_dtype),
        scratch_shapes=[pltpu.VMEM((kc, tn), BF16), pltpu.VMEM((ka, tn), BF16)],
        compiler_params=_params("arbitrary", "arbitrary"),
        name="gated_merge",
    )(c_act, attn, sg, sg, w_conv_out, w_attn_out)


def _router_kernel(y_ref, g_ref, wr_ref, br_ref, hn_ref, idx_ref, gate_ref):
    y = y_ref[...]
    hn = y * lax.rsqrt(jnp.mean(y * y, axis=-1, keepdims=True) + EPS) * g_ref[...]
    hn_ref[...] = hn.astype(hn_ref.dtype)
    logits = jnp.dot(hn, wr_ref[...], precision=lax.Precision.HIGHEST, preferred_element_type=F32) + br_ref[...]
    tm, ne = logits.shape
    lane = lax.broadcasted_iota(I32, (tm, ne), 1).astype(F32)
    out_lane = lax.broadcasted_iota(I32, (tm, LANES), 1)
    idx_out = jnp.zeros((tm, LANES), F32)
    val_out = jnp.full((tm, LANES), NEG, F32)
    l = logits
    for s in range(TOP_K):
        m = jnp.max(l, axis=-1, keepdims=True)
        idx = jnp.min(jnp.where(l == m, lane, float(ne)), axis=-1, keepdims=True)
        idx_out = jnp.where(out_lane == s, idx, idx_out)
        val_out = jnp.where(out_lane == s, m, val_out)
        l = jnp.where(lane == idx, -jnp.inf, l)
    top = jnp.max(val_out, axis=-1, keepdims=True)
    e = jnp.where(out_lane < TOP_K, jnp.exp(val_out - top), 0.0)
    idx_ref[...] = idx_out.astype(I32)
    gate_ref[...] = e / jnp.sum(e, axis=-1, keepdims=True)


def moe_router(y, g, w_router, b_router, *, tm):
    t, d = y.shape
    ne = w_router.shape[1]
    row = lambda i: (0, 0)
    return pl.pallas_call(
        _router_kernel,
        grid=(t // tm,),
        in_specs=[pl.BlockSpec((tm, d), lambda i: (i, 0)), pl.BlockSpec((1, d), row),
                  pl.BlockSpec((d, ne), row), pl.BlockSpec((1, ne), row)],
        out_specs=[pl.BlockSpec((tm, d), lambda i: (i, 0)),
                   pl.BlockSpec((tm, LANES), lambda i: (i, 0)),
                   pl.BlockSpec((tm, LANES), lambda i: (i, 0))],
        out_shape=[jax.ShapeDtypeStruct((t, d), BF16),
                   jax.ShapeDtypeStruct((t, LANES), I32),
                   jax.ShapeDtypeStruct((t, LANES), F32)],
        compiler_params=_params("arbitrary"),
        name="moe_router",
    )(y, g.reshape(1, d), w_router, b_router.reshape(1, ne))


def _expert_tile_state(te_ref, nu_ref):
    i = pl.program_id(1)
    nu = nu_ref[0]
    ic = jnp.minimum(i, nu - 1)
    new_weights = (i == 0) | (te_ref[ic] != te_ref[jnp.maximum(ic - 1, 0)])
    live = i < nu
    return live, live & new_weights


def _moe_up_kernel(te_ref, nu_ref, x_ref, wg_ref, wl_ref, bg_ref, bl_ref, o_ref, wgb_ref, wlb_ref):
    live, recast = _expert_tile_state(te_ref, nu_ref)

    @pl.when(recast)
    def _():
        _cast_rows(wg_ref.at[0], wgb_ref, 256)
        _cast_rows(wl_ref.at[0], wlb_ref, 256)

    @pl.when(live)
    def _():
        x = x_ref[...]
        hg = jnp.dot(x, wgb_ref[...], preferred_element_type=F32) + bg_ref[0]
        hl = jnp.dot(x, wlb_ref[...], preferred_element_type=F32) + bl_ref[0]
        x_glu = jnp.minimum(hg, SWIGLU_LIMIT)
        x_lin = jnp.clip(hl, -SWIGLU_LIMIT, SWIGLU_LIMIT)
        o_ref[...] = (x_glu * jax.nn.sigmoid(SWIGLU_ALPHA * x_glu) * (x_lin + 1.0)).astype(o_ref.dtype)


def _moe_down_kernel(te_ref, nu_ref, a_ref, w_ref, b_ref, g_ref, o_ref, wb_ref):
    live, recast = _expert_tile_state(te_ref, nu_ref)

    @pl.when(recast)
    def _():
        _cast_rows(w_ref.at[0], wb_ref, 256)

    @pl.when(live)
    def _():
        y = jnp.dot(a_ref[...], wb_ref[...], preferred_element_type=F32) + b_ref[0]
        o_ref[...] = (y * g_ref[...]).astype(o_ref.dtype)


def moe_experts(xs, tile_e, n_used, row_g, w_up, b_up, w_down, b_down, *, tm, tn_up, tn_down):
    r, d = xs.shape
    ne, _, two_ff = w_up.shape
    d_ff = two_ff // 2
    n_tiles = r // tm
    row_i = lambda j, i, te, nu: (jnp.minimum(i, nu[0] - 1), 0)
    exp_i = lambda i, te, nu: te[jnp.minimum(i, nu[0] - 1)]
    nj = d_ff // tn_up
    act = pl.pallas_call(
        _moe_up_kernel,
        grid_spec=pltpu.PrefetchScalarGridSpec(
            num_scalar_prefetch=2,
            grid=(nj, n_tiles),
            in_specs=[pl.BlockSpec((tm, d), row_i),
                      pl.BlockSpec((1, d, tn_up), lambda j, i, te, nu: (exp_i(i, te, nu), 0, j)),
                      pl.BlockSpec((1, d, tn_up), lambda j, i, te, nu: (exp_i(i, te, nu), 0, nj + j)),
                      pl.BlockSpec((1, 1, tn_up), lambda j, i, te, nu: (exp_i(i, te, nu), 0, j)),
                      pl.BlockSpec((1, 1, tn_up), lambda j, i, te, nu: (exp_i(i, te, nu), 0, nj + j))],
            out_specs=pl.BlockSpec((tm, tn_up), lambda j, i, te, nu: (jnp.minimum(i, nu[0] - 1), j)),
            scratch_shapes=[pltpu.VMEM((d, tn_up), BF16), pltpu.VMEM((d, tn_up), BF16)]),
        out_shape=jax.ShapeDtypeStruct((r, d_ff), BF16),
        compiler_params=_params("arbitrary", "arbitrary"),
        name="moe_up",
    )(tile_e, n_used, xs, w_up, w_up, b_up.reshape(ne, 1, two_ff), b_up.reshape(ne, 1, two_ff))
    d_out = w_down.shape[2]
    return pl.pallas_call(
        _moe_down_kernel,
        grid_spec=pltpu.PrefetchScalarGridSpec(
            num_scalar_prefetch=2,
            grid=(d_out // tn_down, n_tiles),
            in_specs=[pl.BlockSpec((tm, d_ff), row_i),
                      pl.BlockSpec((1, d_ff, tn_down), lambda j, i, te, nu: (exp_i(i, te, nu), 0, j)),
                      pl.BlockSpec((1, 1, tn_down), lambda j, i, te, nu: (exp_i(i, te, nu), 0, j)),
                      pl.BlockSpec((tm, 1), row_i)],
            out_specs=pl.BlockSpec((tm, tn_down), lambda j, i, te, nu: (jnp.minimum(i, nu[0] - 1), j)),
            scratch_shapes=[pltpu.VMEM((d_ff, tn_down), BF16)]),
        out_shape=jax.ShapeDtypeStruct((r, d_out), F32),
        compiler_params=_params("arbitrary", "arbitrary"),
        name="moe_down",
    )(tile_e, n_used, act, w_down, b_down.reshape(ne, 1, d_out), row_g.reshape(r, 1))


def _route_metadata(top_idx, gates, tm):
    t = top_idx.shape[0]
    tk = t * TOP_K
    flat_e = top_idx.reshape(-1)
    onehot = (flat_e[:, None] == jnp.arange(N_EXPERTS, dtype=I32)[None, :]).astype(I32)
    csum = jnp.cumsum(onehot, axis=0)
    rank = jnp.take_along_axis(csum, flat_e[:, None], axis=1)[:, 0] - 1
    counts = csum[-1]
    padded = (counts + tm - 1) // tm * tm
    pends = jnp.cumsum(padded)
    dest = (pends - padded)[flat_e] + rank
    n_tiles = (tk + N_EXPERTS * (tm - 1) + tm - 1) // tm
    rows = n_tiles * tm
    flat_tok = jnp.arange(tk, dtype=I32) // TOP_K
    row_tok = jnp.zeros((rows,), I32).at[dest].set(flat_tok)
    row_g = jnp.zeros((rows,), F32).at[dest].set(gates.reshape(-1))
    tile_e = jnp.minimum(jnp.searchsorted(pends, jnp.arange(n_tiles, dtype=I32) * tm, side='right'),
                         N_EXPERTS - 1).astype(I32)
    n_used = (pends[-1:] // tm).astype(I32)
    return dest.reshape(t, TOP_K), row_tok, row_g, tile_e, n_used


TILES = dict(tm_light=320, tm=640, tn=512, tn_glu=256, tt_conv=256, tm_e=256, tn_up=512, tn_down=512)


def kernel(x_prompt, x_sample, cache_k, cache_v, state_conv, page_table, attn_norm_g, w_in, conv_w, conv_b,
           conv_ln_g, conv_ln_b, w_conv_out, q_norm_g, k_norm_g, w_attn_out, w_o, ffn_norm_g, w_router,
           b_router, w_up, b_up, w_down, b_down):
    return _step(TILES, x_prompt, x_sample, cache_k, cache_v, state_conv, page_table, attn_norm_g, w_in, conv_w,
                 conv_b, conv_ln_g, conv_ln_b, w_conv_out, q_norm_g, k_norm_g, w_attn_out, w_o, ffn_norm_g,
                 w_router, b_router, w_up, b_up, w_down, b_down)


def _step(cfg, x_prompt, x_sample, cache_k, cache_v, state_conv, page_table, attn_norm_g, w_in, conv_w, conv_b,
          conv_ln_g, conv_ln_b, w_conv_out, q_norm_g, k_norm_g, w_attn_out, w_o, ffn_norm_g, w_router,
          b_router, w_up, b_up, w_down, b_down):
    n_batch, seq_len, d_model = x_prompt.shape
    db, ds, _ = x_sample.shape
    d_conv = conv_w.shape[1]
    d_attn = w_attn_out.shape[0]
    n_heads = d_attn // HEAD_DIM
    past_len = page_table.shape[1] * PAGE_SIZE
    tp = n_batch * seq_len
    ts = db * ds

    x = jnp.concatenate([x_prompt.reshape(tp, d_model), x_sample.reshape(ts, d_model)], axis=0)
    n = rmsnorm_rows(x, attn_norm_g, cfg["tm_light"], BF16)

    proj = functools.partial(dense_proj, n, w_in, tm=cfg["tm"])
    u = proj(0, d_conv, kind="glu", out_dtype=F32, tn=cfg["tn_glu"])
    c = 2 * d_conv
    q = proj(c, d_attn, kind="headnorm", out_dtype=F32, tn=cfg["tn"], extra=q_norm_g)
    k = proj(c + d_attn, d_attn, kind="headnorm", out_dtype=F32, tn=cfg["tn"], extra=k_norm_g)
    v = proj(c + 2 * d_attn, d_attn, kind="none", out_dtype=F32, tn=cfg["tn"])
    sg = proj(c + 3 * d_attn, 2 * d_model, kind="sigmoid", out_dtype=BF16, tn=cfg["tn"])

    conv = functools.partial(conv_branch_act, conv_w=conv_w, conv_b=conv_b, ln_g=conv_ln_g, ln_b=conv_ln_b,
                             out_dtype=BF16)
    c_prompt = conv(u, None, n_seq=n_batch, seq_len=seq_len, tt=cfg["tt_conv"])
    u_s = u[tp:].reshape(db, ds, d_conv)
    ds_pad = -(-ds // 16) * 16
    lead = CONV_HALO - (CONV_WIDTH - 1)
    halo_s = jnp.pad(state_conv.astype(F32), ((0, 0), (lead, 0), (0, 0))).reshape(db * CONV_HALO, d_conv)
    u_s_pad = jnp.pad(u_s, ((0, 0), (0, ds_pad - ds), (0, 0))).reshape(db * ds_pad, d_conv)
    c_sample = conv(u_s_pad, halo_s, n_seq=db, seq_len=ds_pad, tt=ds_pad)
    c_sample = c_sample.reshape(db, ds_pad, d_conv)[:, :ds].reshape(ts, d_conv)
    c_act = jnp.concatenate([c_prompt[:tp], c_sample], axis=0)

    a_prompt = moba_prompt_attention(q, k, v, n_batch=n_batch, seq_len=seq_len, out_dtype=BF16)
    heads = lambda t_: t_[tp:].reshape(db, ds, n_heads, HEAD_DIM)
    a_sample = _sample_attention_jax(heads(q), heads(k), heads(v), cache_k, cache_v, page_table, past_len)
    attn = jnp.concatenate([a_prompt, a_sample.astype(BF16)], axis=0)

    merged = gated_branch_merge(c_act, attn, sg, w_conv_out, w_attn_out, tm=cfg["tm"], tn=cfg["tn"], out_dtype=BF16)
    y1 = dense_proj(merged, w_o, 0, d_model, kind="residual", out_dtype=F32, tm=cfg["tm"], tn=cfg["tn"], extra=x)

    hn, idx_pad, gate_pad = moe_router(y1, ffn_norm_g, w_router, b_router, tm=cfg["tm_light"])
    dest, row_tok, row_g, tile_e, n_used = _route_metadata(idx_pad[:, :TOP_K], gate_pad[:, :TOP_K], cfg["tm_e"])
    xs = jnp.take(hn, row_tok, axis=0)
    yb = moe_experts(xs, tile_e, n_used, row_g, w_up, b_up, w_down, b_down,
                     tm=cfg["tm_e"], tn_up=cfg["tn_up"], tn_down=cfg["tn_down"])
    y = y1 + jnp.sum(jnp.take(yb, dest, axis=0), axis=1)

    u_p = u[:tp].reshape(n_batch, seq_len, d_conv)
    conv_prompt = jnp.pad(u_p, ((0, 0), (CONV_WIDTH - 1, 0), (0, 0)))[:, -(CONV_WIDTH - 1):]
    conv_sample = jnp.concatenate([state_conv.astype(F32), u_s], axis=1)[:, -(CONV_WIDTH - 1):]
    hp = lambda t_: t_[:tp].reshape(n_batch, seq_len, n_heads, HEAD_DIM)
    return (y[:tp].reshape(n_batch, seq_len, d_model), y[tp:].reshape(db, ds, d_model),
            hp(k), hp(v), conv_prompt, heads(k), heads(v), conv_sample)
```

```python
import functools

import jax
import jax.numpy as jnp
from jax import lax
from jax.experimental import pallas as pl
from jax.experimental.pallas import tpu as pltpu

F32 = jnp.float32
BF16 = jnp.bfloat16
I32 = jnp.int32

EPS = 1e-6
NEG = -1e30
HEAD_DIM = 128
MOBA_BLOCK = 256
MOBA_TOPK = 3
PAGE_SIZE = 128
CONV_WIDTH = 31
N_EXPERTS = 32
TOP_K = 4
SWIGLU_ALPHA = 1.702
SWIGLU_LIMIT = 7.0

LANES = 128
CONV_HALO = 32
V7X_VMEM_LIMIT = 56 * 1024 * 1024

_NT = (((1,), (1,)), ((), ()))
_TN = (((0,), (0,)), ((), ()))


def _params(*sem):
    return pltpu.CompilerParams(dimension_semantics=sem, vmem_limit_bytes=V7X_VMEM_LIMIT)


def _cast_rows(src_ref, dst_ref, chunk):
    rows = src_ref.shape[0]
    chunk = min(chunk, rows)

    def body(c, carry):
        r = pl.multiple_of(c * chunk, chunk)
        dst_ref[pl.ds(r, chunk), :] = src_ref[pl.ds(r, chunk), :].astype(dst_ref.dtype)
        return carry

    lax.fori_loop(0, rows // chunk, body, 0)


def _rmsnorm_kernel(x_ref, g_ref, o_ref):
    x = x_ref[...]
    y = x * lax.rsqrt(jnp.mean(x * x, axis=-1, keepdims=True) + EPS)
    o_ref[...] = (y * g_ref[...]).astype(o_ref.dtype)


def rmsnorm_rows(x, g, tm, out_dtype):
    t, d = x.shape
    return pl.pallas_call(
        _rmsnorm_kernel,
        grid=(t // tm,),
        in_specs=[pl.BlockSpec((tm, d), lambda i: (i, 0)), pl.BlockSpec((1, d), lambda i: (0, 0))],
        out_specs=pl.BlockSpec((tm, d), lambda i: (i, 0)),
        out_shape=jax.ShapeDtypeStruct((t, d), out_dtype),
        compiler_params=_params("arbitrary"),
        name="rmsnorm_rows",
    )(x, g.reshape(1, d))


def _proj_kernel(x_ref, *rest, n_w, kind, tn):
    w_refs, rest = rest[:n_w], rest[n_w:]
    if kind in ("headnorm", "residual"):
        e_ref, rest = rest[0], rest[1:]
    o_ref, wb_refs = rest[0], rest[1:]

    @pl.when(pl.program_id(1) == 0)
    def _():
        for w_ref, wb_ref in zip(w_refs, wb_refs):
            _cast_rows(w_ref, wb_ref, 256)

    x = x_ref[...]
    acc = jnp.dot(x, wb_refs[0][...], preferred_element_type=F32)
    if kind == "glu":
        gate = jnp.dot(x, wb_refs[1][...], preferred_element_type=F32)
        o_ref[...] = (acc * jax.nn.sigmoid(gate)).astype(o_ref.dtype)
    elif kind == "headnorm":
        g = e_ref[...]
        for h in range(tn // HEAD_DIM):
            sl = slice(h * HEAD_DIM, (h + 1) * HEAD_DIM)
            blk = acc[:, sl]
            inv = lax.rsqrt(jnp.mean(blk * blk, axis=-1, keepdims=True) + EPS)
            o_ref[:, sl] = (blk * inv * g).astype(o_ref.dtype)
    elif kind == "sigmoid":
        o_ref[...] = jax.nn.sigmoid(acc).astype(o_ref.dtype)
    elif kind == "residual":
        o_ref[...] = (e_ref[...] + acc).astype(o_ref.dtype)
    else:
        o_ref[...] = acc.astype(o_ref.dtype)


def dense_proj(x, w, col0, width, *, kind, out_dtype, tm, tn, extra=None):
    t, k = x.shape
    n_w = 2 if kind == "glu" else 1
    c0 = col0 // tn
    in_specs = [pl.BlockSpec((tm, k), lambda j, i: (i, 0)),
                pl.BlockSpec((k, tn), lambda j, i: (0, c0 + j))]
    args = [x, w]
    if kind == "glu":
        c1 = (col0 + width) // tn
        in_specs.append(pl.BlockSpec((k, tn), lambda j, i: (0, c1 + j)))
        args.append(w)
    if kind == "headnorm":
        in_specs.append(pl.BlockSpec((1, HEAD_DIM), lambda j, i: (0, 0)))
        args.append(extra.reshape(1, HEAD_DIM))
    if kind == "residual":
        in_specs.append(pl.BlockSpec((tm, tn), lambda j, i: (i, j)))
        args.append(extra)
    return pl.pallas_call(
        functools.partial(_proj_kernel, n_w=n_w, kind=kind, tn=tn),
        grid=(width // tn, t // tm),
        in_specs=in_specs,
        out_specs=pl.BlockSpec((tm, tn), lambda j, i: (i, j)),
        out_shape=jax.ShapeDtypeStruct((t, width), out_dtype),
        scratch_shapes=[pltpu.VMEM((k, tn), BF16) for _ in range(n_w)],
        compiler_params=_params("arbitrary", "arbitrary"),
        name="proj_" + kind,
    )(*args)


def _conv_kernel(halo_ref, u_ref, cw_ref, cb_ref, lg_ref, lb_ref, o_ref, ctx_ref, y_ref, *, tt, zero_first_halo,
                 rc, cc):
    if zero_first_halo:
        first = pl.program_id(1) == 0

        @pl.when(first)
        def _():
            ctx_ref[0:CONV_HALO, :] = jnp.zeros((CONV_HALO, ctx_ref.shape[1]), F32)

        @pl.when(jnp.logical_not(first))
        def _():
            ctx_ref[0:CONV_HALO, :] = halo_ref[...]
    else:
        ctx_ref[0:CONV_HALO, :] = halo_ref[...]
    ctx_ref[CONV_HALO:CONV_HALO + tt, :] = u_ref[...]
    d = ctx_ref.shape[1]
    lead = CONV_HALO - (CONV_WIDTH - 1)
    sub = 8

    for c0 in range(0, d, cc):
        def conv_rows(c, carry, c0=c0):
            r = pl.multiple_of(c * rc, rc)
            window = ctx_ref[pl.ds(r, rc + CONV_HALO), c0:c0 + cc]
            acc = jnp.zeros((rc, cc), F32)
            for b in range(sub):
                taps = [s for s in range(lead, lead + CONV_WIDTH) if s % sub == b]
                span = taps[-1] - b + rc
                shifted = window[b:b + span, :]
                for s in taps:
                    w = s - lead
                    acc = acc + shifted[s - b:s - b + rc, :] * cw_ref[w:w + 1, c0:c0 + cc]
            y_ref[pl.ds(r, rc), c0:c0 + cc] = acc + cb_ref[:, c0:c0 + cc]
            return carry

        lax.fori_loop(0, tt // rc, conv_rows, 0)

    lg = lg_ref[...]
    lb = lb_ref[...]

    def norm_rows(c, carry):
        r = pl.multiple_of(c * rc, rc)
        y = y_ref[pl.ds(r, rc), :]
        mu = jnp.mean(y, axis=-1, keepdims=True)
        yc = y - mu
        var = jnp.mean(yc * yc, axis=-1, keepdims=True)
        z = yc * lax.rsqrt(var + EPS) * lg + lb
        o_ref[pl.ds(r, rc), :] = (z * jax.nn.sigmoid(z)).astype(o_ref.dtype)
        return carry

    lax.fori_loop(0, tt // rc, norm_rows, 0)


def conv_branch_act(u, halo, conv_w, conv_b, ln_g, ln_b, *, n_seq, seq_len, tt, out_dtype):
    d = u.shape[1]
    steps = seq_len // tt
    zero_first = halo is None
    if zero_first:
        hb = tt // CONV_HALO
        halo_arr = u
        halo_spec = pl.BlockSpec((CONV_HALO, d), lambda b, t: (jnp.maximum((b * steps + t) * hb - 1, 0), 0))
    else:
        halo_arr = halo
        halo_spec = pl.BlockSpec((CONV_HALO, d), lambda b, t: (b, 0))
    row = lambda b, t: (0, 0)
    return pl.pallas_call(
        functools.partial(_conv_kernel, tt=tt, zero_first_halo=zero_first, rc=16, cc=min(256, d)),
        grid=(n_seq, steps),
        in_specs=[halo_spec,
                  pl.BlockSpec((tt, d), lambda b, t: (b * steps + t, 0)),
                  pl.BlockSpec((CONV_WIDTH, d), row),
                  pl.BlockSpec((1, d), row), pl.BlockSpec((1, d), row), pl.BlockSpec((1, d), row)],
        out_specs=pl.BlockSpec((tt, d), lambda b, t: (b * steps + t, 0)),
        out_shape=jax.ShapeDtypeStruct((n_seq * seq_len, d), out_dtype),
        scratch_shapes=[pltpu.VMEM((CONV_HALO + tt, d), F32), pltpu.VMEM((tt, d), F32)],
        compiler_params=_params("arbitrary", "arbitrary"),
        name="conv_ln_silu",
    )(halo_arr, u, conv_w, conv_b.reshape(1, d), ln_g.reshape(1, d), ln_b.reshape(1, d))


def _moba_prompt_kernel(q_ref, k_ref, v_ref, o_ref, kb_ref, vb_ref, km_ref, *, nb):
    blk = MOBA_BLOCK
    j = pl.program_id(2)

    @pl.when(j == 0)
    def _():
        for n in range(nb):
            kf = k_ref[n * blk:(n + 1) * blk, :]
            kb_ref[n * blk:(n + 1) * blk, :] = kf.astype(BF16)
            vb_ref[n * blk:(n + 1) * blk, :] = v_ref[n * blk:(n + 1) * blk, :].astype(BF16)
            km_ref[n:n + 1, :] = jnp.mean(kf, axis=0, keepdims=True)

    qf = q_ref[...]
    sc = lax.dot_general(km_ref[...], qf, _NT, precision=lax.Precision.HIGHEST, preferred_element_type=F32)
    bi = lax.broadcasted_iota(I32, (nb, blk), 0)
    rank = jnp.zeros((nb, blk), I32)
    for m in range(nb):
        row = sc[m:m + 1, :]
        beats = (row > sc) | ((row == sc) & (m < bi))
        rank = rank + jnp.where(beats, (m < j).astype(I32), 0)
    sel = jnp.where((bi < j) & (rank < MOBA_TOPK), 1.0, 0.0)

    qb = qf.astype(BF16)
    ki = lax.broadcasted_iota(I32, (blk, blk), 0)
    qi = lax.broadcasted_iota(I32, (blk, blk), 1)
    causal = jnp.where(ki <= qi, 1.0, 0.0)
    scale = HEAD_DIM ** -0.5

    def body(n, carry):
        m_run, l_run, acc = carry
        r = pl.multiple_of(n * blk, blk)
        s = lax.dot_general(kb_ref[pl.ds(r, blk), :], qb, _NT, preferred_element_type=F32) * scale
        selrow = jnp.max(jnp.where(bi == n, sel, 0.0), axis=0, keepdims=True)
        own = (n == j).astype(F32)
        mask = (own * causal + (1.0 - own) * selrow) > 0.5
        s = jnp.where(mask, s, NEG)
        m_new = jnp.maximum(m_run, jnp.max(s, axis=0, keepdims=True))
        p = jnp.where(mask, jnp.exp(s - m_new), 0.0)
        alpha = jnp.exp(m_run - m_new)
        l_new = alpha * l_run + jnp.sum(p, axis=0, keepdims=True)
        pv = lax.dot_general(vb_ref[pl.ds(r, blk), :], p.astype(BF16), _TN, preferred_element_type=F32)
        return m_new, l_new, alpha * acc + pv

    init = (jnp.full((1, blk), NEG, F32), jnp.zeros((1, blk), F32), jnp.zeros((HEAD_DIM, blk), F32))
    _, l_run, acc = lax.fori_loop(0, j + 1, body, init)
    o_ref[...] = (acc / l_run).T.astype(o_ref.dtype)


def moba_prompt_attention(q, k, v, *, n_batch, seq_len, out_dtype):
    assert seq_len % MOBA_BLOCK == 0
    nb = seq_len // MOBA_BLOCK
    n_heads = q.shape[1] // HEAD_DIM
    qo_spec = pl.BlockSpec((MOBA_BLOCK, HEAD_DIM), lambda b, h, j: (b * nb + j, h))
    kv_spec = pl.BlockSpec((seq_len, HEAD_DIM), lambda b, h, j: (b, h))
    return pl.pallas_call(
        functools.partial(_moba_prompt_kernel, nb=nb),
        grid=(n_batch, n_heads, nb),
        in_specs=[qo_spec, kv_spec, kv_spec],
        out_specs=qo_spec,
        out_shape=jax.ShapeDtypeStruct((n_batch * seq_len, n_heads * HEAD_DIM), out_dtype),
        scratch_shapes=[pltpu.VMEM((seq_len, HEAD_DIM), BF16), pltpu.VMEM((seq_len, HEAD_DIM), BF16),
                        pltpu.VMEM((nb, HEAD_DIM), F32)],
        compiler_params=_params("arbitrary", "arbitrary", "arbitrary"),
        name="moba_prompt",
    )(q, k, v)


def _moba_sample_kernel(pt_ref, q_ref, ko_ref, vo_ref, k0_ref, k1_ref, v0_ref, v1_ref, o_ref,
                        bias_ref, acc_ref, m_ref, l_ref, sc_ref, *, n_blocks, ds, n_heads):
    b = pl.program_id(0)
    n = pl.program_id(1)
    rows = ds * n_heads
    page_rows = PAGE_SIZE * n_heads
    scale = HEAD_DIM ** -0.5

    @pl.when((b == 0) & (n == 0))
    def _():
        ci = lax.broadcasted_iota(I32, (rows, page_rows), 0)
        ri = lax.broadcasted_iota(I32, (rows, page_rows), 1)
        bias_ref[...] = jnp.where((ci % n_heads) == (ri % n_heads), 0.0, NEG)

    qf = q_ref[0]
    qb = qf.astype(BF16)
    for i, (k_ref, v_ref) in enumerate(((k0_ref, v0_ref), (k1_ref, v1_ref))):
        kf = k_ref[0]
        ksum = jnp.sum(kf.reshape(PAGE_SIZE, n_heads, HEAD_DIM), axis=0)
        ksum_rows = jnp.concatenate([ksum] * ds, axis=0)
        score = jnp.sum(qf * ksum_rows, axis=-1, keepdims=True)
        s = lax.dot_general(qb, kf.astype(BF16), _NT, preferred_element_type=F32) * scale + bias_ref[...]
        m = jnp.max(s, axis=-1, keepdims=True)
        p = jnp.exp(s - m)
        l = jnp.sum(p, axis=-1, keepdims=True)
        acc = jnp.dot(p.astype(BF16), v_ref[0].astype(BF16), preferred_element_type=F32)
        page = 2 * n + i
        acc_ref[page] = acc
        m_ref[page] = jnp.broadcast_to(m, (rows, HEAD_DIM))
        l_ref[page] = jnp.broadcast_to(l, (rows, HEAD_DIM))
        sc_ref[page] = jnp.broadcast_to(score, (rows, HEAD_DIM))

    @pl.when(n == n_blocks - 1)
    def _():
        n_pages = 2 * n_blocks
        shape = (n_blocks, rows, HEAD_DIM)
        scb = jnp.sum(sc_ref[...].reshape(n_blocks, 2, rows, HEAD_DIM), axis=1)
        ni = lax.broadcasted_iota(I32, shape, 0).astype(F32)
        selb = jnp.zeros(shape, F32)
        for _ in range(MOBA_TOPK):
            mx = jnp.max(scb, axis=0, keepdims=True)
            idx = jnp.min(jnp.where(scb == mx, ni, float(n_blocks)), axis=0, keepdims=True)
            hit = ni == idx
            selb = jnp.where(hit, 1.0, selb)
            scb = jnp.where(hit, -jnp.inf, scb)
        selp = jnp.broadcast_to(selb[:, None], (n_blocks, 2, rows, HEAD_DIM)).reshape(n_pages, rows, HEAD_DIM) > 0.5
        mp = m_ref[...]
        top = jnp.max(jnp.where(selp, mp, NEG), axis=0)
        ko = ko_ref[0]
        vo = vo_ref[0]
        q_idx = lax.broadcasted_iota(I32, (rows, HEAD_DIM), 0) // n_heads
        s_own = []
        for t in range(ds):
            kt = jnp.concatenate([ko[t * n_heads:(t + 1) * n_heads]] * ds, axis=0)
            st = jnp.sum(qf * kt, axis=-1, keepdims=True) * scale
            st = jnp.where(q_idx >= t, st, NEG)
            s_own.append(st)
            top = jnp.maximum(top, st)
        w = jnp.where(selp, jnp.exp(mp - top[None]), 0.0)
        den = jnp.sum(w * l_ref[...], axis=0)
        num = jnp.sum(w * acc_ref[...], axis=0)
        for t in range(ds):
            pt = jnp.where(q_idx >= t, jnp.exp(s_own[t] - top), 0.0)
            vt = jnp.concatenate([vo[t * n_heads:(t + 1) * n_heads]] * ds, axis=0)
            den = den + pt
            num = num + pt * vt
        o_ref[0] = num / den


def moba_sample_attention(q, k, v, cache_k, cache_v, page_table):
    db, ds, n_heads, _ = q.shape
    n_pages = page_table.shape[1]
    assert MOBA_BLOCK == 2 * PAGE_SIZE and n_pages % 2 == 0 and ds <= PAGE_SIZE
    n_blocks = n_pages // 2
    assert n_blocks >= MOBA_TOPK
    rows = ds * n_heads
    page_rows = PAGE_SIZE * n_heads
    n_phys = cache_k.shape[0]
    flat = lambda t: t.reshape(db, rows, HEAD_DIM)
    pages = lambda c: c.reshape(n_phys, page_rows, HEAD_DIM)
    row_spec = pl.BlockSpec((1, rows, HEAD_DIM), lambda b, n, pt: (b, 0, 0))

    def page_spec(i):
        return pl.BlockSpec((1, page_rows, HEAD_DIM), lambda b, n, pt: (pt[b * n_pages + 2 * n + i], 0, 0))

    part = pltpu.VMEM((n_pages, rows, HEAD_DIM), F32)
    out = pl.pallas_call(
        functools.partial(_moba_sample_kernel, n_blocks=n_blocks, ds=ds, n_heads=n_heads),
        grid_spec=pltpu.PrefetchScalarGridSpec(
            num_scalar_prefetch=1,
            grid=(db, n_blocks),
            in_specs=[row_spec, row_spec, row_spec, page_spec(0), page_spec(1), page_spec(0), page_spec(1)],
            out_specs=row_spec,
            scratch_shapes=[pltpu.VMEM((rows, page_rows), F32), part, part, part, part]),
        out_shape=jax.ShapeDtypeStruct((db, rows, HEAD_DIM), F32),
        compiler_params=_params("arbitrary", "arbitrary"),
        name="moba_sample",
    )(page_table.reshape(-1), flat(q), flat(k), flat(v), pages(cache_k), pages(cache_k), pages(cache_v),
      pages(cache_v))
    return out.reshape(db * ds, n_heads * HEAD_DIM)


def _merge_kernel(c_ref, a_ref, gc_ref, ga_ref, wc_ref, wa_ref, o_ref, wcb_ref, wab_ref):
    @pl.when(pl.program_id(1) == 0)
    def _():
        _cast_rows(wc_ref, wcb_ref, 256)
        _cast_rows(wa_ref, wab_ref, 256)

    yc = jnp.dot(c_ref[...], wcb_ref[...], preferred_element_type=F32)
    ya = jnp.dot(a_ref[...], wab_ref[...], preferred_element_type=F32)
    o_ref[...] = (gc_ref[...].astype(F32) * yc + ga_ref[...].astype(F32) * ya).astype(o_ref.dtype)


def gated_branch_merge(c_act, attn, sg, w_conv_out, w_attn_out, *, tm, tn, out_dtype):
    t, kc = c_act.shape
    ka = attn.shape[1]
    n = w_conv_out.shape[1]
    nj = n // tn
    return pl.pallas_call(
        _merge_kernel,
        grid=(nj, t // tm),
        in_specs=[pl.BlockSpec((tm, kc), lambda j, i: (i, 0)),
                  pl.BlockSpec((tm, ka), lambda j, i: (i, 0)),
                  pl.BlockSpec((tm, tn), lambda j, i: (i, j)),
                  pl.BlockSpec((tm, tn), lambda j, i: (i, nj + j)),
                  pl.BlockSpec((kc, tn), lambda j, i: (0, j)),
                  pl.BlockSpec((ka, tn), lambda j, i: (0, j))],
        out_specs=pl.BlockSpec((tm, tn), lambda j, i: (i, j)),
        out_shape=jax.ShapeDtypeStruct((t, n), out_dtype),
        scratch_shapes=[pltpu.VMEM((kc, tn), BF16), pltpu.VMEM((ka, tn), BF16)],
        compiler_params=_params("arbitrary", "arbitrary"),
        name="gated_merge",
    )(c_act, attn, sg, sg, w_conv_out, w_attn_out)


def _router_kernel(y_ref, g_ref, wr_ref, br_ref, idx_ref, gate_ref):
    y = y_ref[...]
    hn = y * lax.rsqrt(jnp.mean(y * y, axis=-1, keepdims=True) + EPS) * g_ref[...]
    logits = jnp.dot(hn, wr_ref[...], precision=lax.Precision.HIGHEST, preferred_element_type=F32) + br_ref[...]
    tm, ne = logits.shape
    lane = lax.broadcasted_iota(I32, (tm, ne), 1).astype(F32)
    out_lane = lax.broadcasted_iota(I32, (tm, LANES), 1)
    idx_out = jnp.zeros((tm, LANES), F32)
    val_out = jnp.full((tm, LANES), NEG, F32)
    l = logits
    for s in range(TOP_K):
        m = jnp.max(l, axis=-1, keepdims=True)
        idx = jnp.min(jnp.where(l == m, lane, float(ne)), axis=-1, keepdims=True)
        idx_out = jnp.where(out_lane == s, idx, idx_out)
        val_out = jnp.where(out_lane == s, m, val_out)
        l = jnp.where(lane == idx, -jnp.inf, l)
    top = jnp.max(val_out, axis=-1, keepdims=True)
    e = jnp.where(out_lane < TOP_K, jnp.exp(val_out - top), 0.0)
    idx_ref[...] = idx_out.astype(I32)
    gate_ref[...] = e / jnp.sum(e, axis=-1, keepdims=True)


def moe_router(y, g, w_router, b_router, *, tm):
    t, d = y.shape
    ne = w_router.shape[1]
    row = lambda i: (0, 0)
    return pl.pallas_call(
        _router_kernel,
        grid=(t // tm,),
        in_specs=[pl.BlockSpec((tm, d), lambda i: (i, 0)), pl.BlockSpec((1, d), row),
                  pl.BlockSpec((d, ne), row), pl.BlockSpec((1, ne), row)],
        out_specs=[pl.BlockSpec((tm, LANES), lambda i: (i, 0)),
                   pl.BlockSpec((tm, LANES), lambda i: (i, 0))],
        out_shape=[jax.ShapeDtypeStruct((t, LANES), I32),
                   jax.ShapeDtypeStruct((t, LANES), F32)],
        compiler_params=_params("arbitrary"),
        name="moe_router",
    )(y, g.reshape(1, d), w_router, b_router.reshape(1, ne))


def _row_gather_start(idx_ref, n_rows, src_hbm, buf_ref, sem_ref, slot):
    def body(r, carry):
        row = idx_ref[0, 0, r]
        pltpu.make_async_copy(src_hbm.at[pl.ds(row, 1), :], buf_ref.at[slot, pl.ds(r, 1), :], sem_ref.at[slot]).start()
        return carry

    lax.fori_loop(0, n_rows, body, 0, unroll=8)


def _row_gather_wait(n_rows, src_hbm, buf_ref, sem_ref, slot):
    pltpu.make_async_copy(src_hbm.at[pl.ds(0, n_rows), :], buf_ref.at[slot], sem_ref.at[slot]).wait()


def _dispatch_kernel(cr_ref, idx_ref, idx_next_ref, y_hbm, g_ref, o_ref, buf_ref, sem_ref, *, tg):
    c = pl.program_id(0)
    n_chunks = pl.num_programs(0)
    slot = c % 2
    valid = cr_ref[c] > 0

    @pl.when((c == 0) & valid)
    def _():
        _row_gather_start(idx_ref, tg, y_hbm, buf_ref, sem_ref, 0)

    @pl.when((c + 1 < n_chunks) & (cr_ref[jnp.minimum(c + 1, n_chunks - 1)] > 0))
    def _():
        _row_gather_start(idx_next_ref, tg, y_hbm, buf_ref, sem_ref, 1 - slot)

    @pl.when(valid)
    def _():
        _row_gather_wait(tg, y_hbm, buf_ref, sem_ref, slot)
        y = buf_ref[slot]
        hn = y * lax.rsqrt(jnp.mean(y * y, axis=-1, keepdims=True) + EPS) * g_ref[...]
        o_ref[...] = hn.astype(o_ref.dtype)

    @pl.when(jnp.logical_not(valid))
    def _():
        o_ref[...] = jnp.zeros(o_ref.shape, o_ref.dtype)


def moe_dispatch(y, g, row_tok, chunk_rows, *, tg):
    t, d = y.shape
    n_chunks = row_tok.shape[0] // tg
    idx = row_tok.reshape(n_chunks, 1, tg)
    return pl.pallas_call(
        functools.partial(_dispatch_kernel, tg=tg),
        grid_spec=pltpu.PrefetchScalarGridSpec(
            num_scalar_prefetch=1,
            grid=(n_chunks,),
            in_specs=[pl.BlockSpec((1, 1, tg), lambda c, cr: (c, 0, 0), memory_space=pltpu.SMEM),
                      pl.BlockSpec((1, 1, tg), lambda c, cr: (jnp.minimum(c + 1, n_chunks - 1), 0, 0),
                                   memory_space=pltpu.SMEM),
                      pl.BlockSpec(memory_space=pl.ANY),
                      pl.BlockSpec((1, d), lambda c, cr: (0, 0))],
            out_specs=pl.BlockSpec((tg, d), lambda c, cr: (c, 0)),
            scratch_shapes=[pltpu.VMEM((2, tg, d), F32), pltpu.SemaphoreType.DMA((2,))]),
        out_shape=jax.ShapeDtypeStruct((n_chunks * tg, d), BF16),
        compiler_params=_params("arbitrary"),
        name="moe_dispatch",
    )(chunk_rows, idx, idx, y, g.reshape(1, d))


def _expert_tile_state(te_ref, tr_ref, nu_ref):
    i = pl.program_id(1)
    nu = nu_ref[0]
    ic = jnp.minimum(i, nu - 1)
    new_weights = (i == 0) | (te_ref[ic] != te_ref[jnp.maximum(ic - 1, 0)])
    live = i < nu
    return live, live & new_weights, tr_ref[ic]


def _for_live_rows(live, n_valid, tm, o_ref, compute):
    half = tm // 2

    @pl.when(jnp.logical_not(live))
    def _():
        o_ref[...] = jnp.zeros(o_ref.shape, o_ref.dtype)

    @pl.when(live & (n_valid > half))
    def _():
        compute(0, tm)

    @pl.when(live & (n_valid <= half))
    def _():
        compute(0, half)
        o_ref[half:tm, :] = jnp.zeros((tm - half, o_ref.shape[1]), o_ref.dtype)


def _moe_up_kernel(te_ref, tr_ref, nu_ref, x_ref, wg_ref, wl_ref, bg_ref, bl_ref, o_ref, wgb_ref, wlb_ref, *, tm):
    live, recast, n_valid = _expert_tile_state(te_ref, tr_ref, nu_ref)

    @pl.when(recast)
    def _():
        _cast_rows(wg_ref.at[0], wgb_ref, 256)
        _cast_rows(wl_ref.at[0], wlb_ref, 256)

    def compute(r0, r1):
        x = x_ref[r0:r1, :]
        hg = jnp.dot(x, wgb_ref[...], preferred_element_type=F32) + bg_ref[0]
        hl = jnp.dot(x, wlb_ref[...], preferred_element_type=F32) + bl_ref[0]
        x_glu = jnp.minimum(hg, SWIGLU_LIMIT)
        x_lin = jnp.clip(hl, -SWIGLU_LIMIT, SWIGLU_LIMIT)
        o_ref[r0:r1, :] = (x_glu * jax.nn.sigmoid(SWIGLU_ALPHA * x_glu) * (x_lin + 1.0)).astype(o_ref.dtype)

    _for_live_rows(live, n_valid, tm, o_ref, compute)


def _moe_down_kernel(te_ref, tr_ref, nu_ref, a_ref, w_ref, b_ref, g_ref, o_ref, wb_ref, *, tm):
    live, recast, n_valid = _expert_tile_state(te_ref, tr_ref, nu_ref)

    @pl.when(recast)
    def _():
        _cast_rows(w_ref.at[0], wb_ref, 256)

    def compute(r0, r1):
        y = jnp.dot(a_ref[r0:r1, :], wb_ref[...], preferred_element_type=F32) + b_ref[0]
        o_ref[r0:r1, :] = (y * g_ref[r0:r1, :]).astype(o_ref.dtype)

    _for_live_rows(live, n_valid, tm, o_ref, compute)


def moe_experts(xs, tile_e, tile_rows, n_used, row_g, w_up, b_up, w_down, b_down, *, tm, tn_up, tn_down):
    r, d = xs.shape
    ne, _, two_ff = w_up.shape
    d_ff = two_ff // 2
    n_tiles = r // tm
    row_i = lambda j, i, te, tr, nu: (jnp.minimum(i, nu[0] - 1), 0)
    exp_i = lambda i, te, nu: te[jnp.minimum(i, nu[0] - 1)]
    nj = d_ff // tn_up
    act = pl.pallas_call(
        functools.partial(_moe_up_kernel, tm=tm),
        grid_spec=pltpu.PrefetchScalarGridSpec(
            num_scalar_prefetch=3,
            grid=(nj, n_tiles),
            in_specs=[pl.BlockSpec((tm, d), row_i),
                      pl.BlockSpec((1, d, tn_up), lambda j, i, te, tr, nu: (exp_i(i, te, nu), 0, j)),
                      pl.BlockSpec((1, d, tn_up), lambda j, i, te, tr, nu: (exp_i(i, te, nu), 0, nj + j)),
                      pl.BlockSpec((1, 1, tn_up), lambda j, i, te, tr, nu: (exp_i(i, te, nu), 0, j)),
                      pl.BlockSpec((1, 1, tn_up), lambda j, i, te, tr, nu: (exp_i(i, te, nu), 0, nj + j))],
            out_specs=pl.BlockSpec((tm, tn_up), lambda j, i, te, tr, nu: (i, j)),
            scratch_shapes=[pltpu.VMEM((d, tn_up), BF16), pltpu.VMEM((d, tn_up), BF16)]),
        out_shape=jax.ShapeDtypeStruct((r, d_ff), BF16),
        compiler_params=_params("arbitrary", "arbitrary"),
        name="moe_up",
    )(tile_e, tile_rows, n_used, xs, w_up, w_up, b_up.reshape(ne, 1, two_ff), b_up.reshape(ne, 1, two_ff))
    d_out = w_down.shape[2]
    return pl.pallas_call(
        functools.partial(_moe_down_kernel, tm=tm),
        grid_spec=pltpu.PrefetchScalarGridSpec(
            num_scalar_prefetch=3,
            grid=(d_out // tn_down, n_tiles),
            in_specs=[pl.BlockSpec((tm, d_ff), row_i),
                      pl.BlockSpec((1, d_ff, tn_down), lambda j, i, te, tr, nu: (exp_i(i, te, nu), 0, j)),
                      pl.BlockSpec((1, 1, tn_down), lambda j, i, te, tr, nu: (exp_i(i, te, nu), 0, j)),
                      pl.BlockSpec((tm, 1), row_i)],
            out_specs=pl.BlockSpec((tm, tn_down), lambda j, i, te, tr, nu: (i, j)),
            scratch_shapes=[pltpu.VMEM((d_ff, tn_down), BF16)]),
        out_shape=jax.ShapeDtypeStruct((r, d_out), F32),
        compiler_params=_params("arbitrary", "arbitrary"),
        name="moe_down",
    )(tile_e, tile_rows, n_used, act, w_down, b_down.reshape(ne, 1, d_out), row_g.reshape(r, 1))


def _combine_kernel(idx_ref, idx_next_ref, y_ref, yb_hbm, o_ref, buf_ref, sem_ref, *, tq):
    i = pl.program_id(0)
    n_steps = pl.num_programs(0)
    slot = i % 2
    n_rows = TOP_K * tq

    @pl.when(i == 0)
    def _():
        _row_gather_start(idx_ref, n_rows, yb_hbm, buf_ref, sem_ref, 0)

    @pl.when(i + 1 < n_steps)
    def _():
        _row_gather_start(idx_next_ref, n_rows, yb_hbm, buf_ref, sem_ref, 1 - slot)

    _row_gather_wait(n_rows, yb_hbm, buf_ref, sem_ref, slot)
    acc = y_ref[...]
    for s in range(TOP_K):
        acc = acc + buf_ref[slot, s * tq:(s + 1) * tq, :]
    o_ref[...] = acc


def moe_combine(y, yb, dest, *, row0, n_rows, tq):
    d = y.shape[1]
    n_steps = n_rows // tq
    b0 = row0 // tq
    idx = dest[row0:row0 + n_rows].reshape(n_steps, tq, TOP_K).transpose(0, 2, 1).reshape(n_steps, 1, TOP_K * tq)
    return pl.pallas_call(
        functools.partial(_combine_kernel, tq=tq),
        grid=(n_steps,),
        in_specs=[pl.BlockSpec((1, 1, TOP_K * tq), lambda i: (i, 0, 0), memory_space=pltpu.SMEM),
                  pl.BlockSpec((1, 1, TOP_K * tq), lambda i: (jnp.minimum(i + 1, n_steps - 1), 0, 0),
                               memory_space=pltpu.SMEM),
                  pl.BlockSpec((tq, d), lambda i: (b0 + i, 0)),
                  pl.BlockSpec(memory_space=pl.ANY)],
        out_specs=pl.BlockSpec((tq, d), lambda i: (i, 0)),
        out_shape=jax.ShapeDtypeStruct((n_rows, d), F32),
        scratch_shapes=[pltpu.VMEM((2, TOP_K * tq, d), F32), pltpu.SemaphoreType.DMA((2,))],
        compiler_params=_params("arbitrary"),
        name="moe_combine",
    )(idx, idx, y, yb)


def _route_metadata(top_idx, gates, tm, tg):
    t = top_idx.shape[0]
    tk = t * TOP_K
    flat_e = top_idx.reshape(-1)
    onehot = (flat_e[:, None] == jnp.arange(N_EXPERTS, dtype=I32)[None, :]).astype(I32)
    csum = jnp.cumsum(onehot, axis=0)
    rank = jnp.take_along_axis(csum, flat_e[:, None], axis=1)[:, 0] - 1
    counts = csum[-1]
    padded = (counts + tm - 1) // tm * tm
    pends = jnp.cumsum(padded)
    pstarts = pends - padded
    dest = pstarts[flat_e] + rank
    n_tiles = (tk + N_EXPERTS * (tm - 1) + tm - 1) // tm
    rows = n_tiles * tm
    flat_tok = jnp.arange(tk, dtype=I32) // TOP_K
    row_tok = jnp.zeros((rows,), I32).at[dest].set(flat_tok)
    row_g = jnp.zeros((rows,), F32).at[dest].set(gates.reshape(-1))
    tile_start = jnp.arange(n_tiles, dtype=I32) * tm
    tile_e = jnp.minimum(jnp.searchsorted(pends, tile_start, side='right'), N_EXPERTS - 1).astype(I32)
    tile_rows = jnp.clip(counts[tile_e] - (tile_start - pstarts[tile_e]), 0, tm).astype(I32)
    per = tm // tg
    chunk_rows = jnp.clip(jnp.repeat(tile_rows, per) - jnp.tile(jnp.arange(per, dtype=I32) * tg, n_tiles), 0, tg)
    n_used = (pends[-1:] // tm).astype(I32)
    return dest.reshape(t, TOP_K), row_tok, row_g, tile_e, tile_rows, chunk_rows.astype(I32), n_used


TILES = dict(tm_light=320, tm=640, tn=512, tn_glu=256, tt_conv=256, tm_e=512, tg=256, tq=128, tn_up=512,
             tn_down=512)


def kernel(x_prompt, x_sample, cache_k, cache_v, state_conv, page_table, attn_norm_g, w_in, conv_w, conv_b,
           conv_ln_g, conv_ln_b, w_conv_out, q_norm_g, k_norm_g, w_attn_out, w_o, ffn_norm_g, w_router,
           b_router, w_up, b_up, w_down, b_down):
    return _step(TILES, x_prompt, x_sample, cache_k, cache_v, state_conv, page_table, attn_norm_g, w_in, conv_w,
                 conv_b, conv_ln_g, conv_ln_b, w_conv_out, q_norm_g, k_norm_g, w_attn_out, w_o, ffn_norm_g,
                 w_router, b_router, w_up, b_up, w_down, b_down)


def _step(cfg, x_prompt, x_sample, cache_k, cache_v, state_conv, page_table, attn_norm_g, w_in, conv_w, conv_b,
          conv_ln_g, conv_ln_b, w_conv_out, q_norm_g, k_norm_g, w_attn_out, w_o, ffn_norm_g, w_router,
          b_router, w_up, b_up, w_down, b_down):
    n_batch, seq_len, d_model = x_prompt.shape
    db, ds, _ = x_sample.shape
    d_conv = conv_w.shape[1]
    d_attn = w_attn_out.shape[0]
    n_heads = d_attn // HEAD_DIM
    past_len = page_table.shape[1] * PAGE_SIZE
    tp = n_batch * seq_len
    ts = db * ds

    x = jnp.concatenate([x_prompt.reshape(tp, d_model), x_sample.reshape(ts, d_model)], axis=0)
    n = rmsnorm_rows(x, attn_norm_g, cfg["tm_light"], BF16)

    proj = functools.partial(dense_proj, n, w_in, tm=cfg["tm"])
    u = proj(0, d_conv, kind="glu", out_dtype=F32, tn=cfg["tn_glu"])
    c = 2 * d_conv
    q = proj(c, d_attn, kind="headnorm", out_dtype=F32, tn=cfg["tn"], extra=q_norm_g)
    k = proj(c + d_attn, d_attn, kind="headnorm", out_dtype=F32, tn=cfg["tn"], extra=k_norm_g)
    v = proj(c + 2 * d_attn, d_attn, kind="none", out_dtype=F32, tn=cfg["tn"])
    sg = proj(c + 3 * d_attn, 2 * d_model, kind="sigmoid", out_dtype=BF16, tn=cfg["tn"])

    conv = functools.partial(conv_branch_act, conv_w=conv_w, conv_b=conv_b, ln_g=conv_ln_g, ln_b=conv_ln_b,
                             out_dtype=BF16)
    c_prompt = conv(u, None, n_seq=n_batch, seq_len=seq_len, tt=cfg["tt_conv"])
    u_s = u[tp:].reshape(db, ds, d_conv)
    ds_pad = -(-ds // 16) * 16
    lead = CONV_HALO - (CONV_WIDTH - 1)
    halo_s = jnp.pad(state_conv.astype(F32), ((0, 0), (lead, 0), (0, 0))).reshape(db * CONV_HALO, d_conv)
    u_s_pad = jnp.pad(u_s, ((0, 0), (0, ds_pad - ds), (0, 0))).reshape(db * ds_pad, d_conv)
    c_sample = conv(u_s_pad, halo_s, n_seq=db, seq_len=ds_pad, tt=ds_pad)
    c_sample = c_sample.reshape(db, ds_pad, d_conv)[:, :ds].reshape(ts, d_conv)
    c_act = jnp.concatenate([c_prompt[:tp], c_sample], axis=0)

    a_prompt = moba_prompt_attention(q, k, v, n_batch=n_batch, seq_len=seq_len, out_dtype=BF16)
    heads = lambda t_: t_[tp:].reshape(db, ds, n_heads, HEAD_DIM)
    a_sample = moba_sample_attention(heads(q), heads(k), heads(v), cache_k, cache_v, page_table)
    attn = jnp.concatenate([a_prompt, a_sample.astype(BF16)], axis=0)

    merged = gated_branch_merge(c_act, attn, sg, w_conv_out, w_attn_out, tm=cfg["tm"], tn=cfg["tn"], out_dtype=BF16)
    y1 = dense_proj(merged, w_o, 0, d_model, kind="residual", out_dtype=F32, tm=cfg["tm"], tn=cfg["tn"], extra=x)

    idx_pad, gate_pad = moe_router(y1, ffn_norm_g, w_router, b_router, tm=cfg["tm_light"])
    dest, row_tok, row_g, tile_e, tile_rows, chunk_rows, n_used = _route_metadata(
        idx_pad[:, :TOP_K], gate_pad[:, :TOP_K], cfg["tm_e"], cfg["tg"])
    xs = moe_dispatch(y1, ffn_norm_g, row_tok, chunk_rows, tg=cfg["tg"])
    yb = moe_experts(xs, tile_e, tile_rows, n_used, row_g, w_up, b_up, w_down, b_down,
                     tm=cfg["tm_e"], tn_up=cfg["tn_up"], tn_down=cfg["tn_down"])
    y_p = moe_combine(y1, yb, dest, row0=0, n_rows=tp, tq=cfg["tq"])
    y_s = moe_combine(y1, yb, dest, row0=tp, n_rows=ts, tq=cfg["tq"])

    u_p = u[:tp].reshape(n_batch, seq_len, d_conv)
    conv_prompt = jnp.pad(u_p, ((0, 0), (CONV_WIDTH - 1, 0), (0, 0)))[:, -(CONV_WIDTH - 1):]
    conv_sample = jnp.concatenate([state_conv.astype(F32), u_s], axis=1)[:, -(CONV_WIDTH - 1):]
    hp = lambda t_: t_[:tp].reshape(n_batch, seq_len, n_heads, HEAD_DIM)
    return (y_p.reshape(n_batch, seq_len, d_model), y_s.reshape(db, ds, d_model),
            hp(k), hp(v), conv_prompt, heads(k), heads(v), conv_sample)
```

```python
import functools

import jax
import jax.numpy as jnp
from jax import lax
from jax.experimental import pallas as pl
from jax.experimental.pallas import tpu as pltpu

F32 = jnp.float32
BF16 = jnp.bfloat16
I32 = jnp.int32

EPS = 1e-6
NEG = -1e30
HEAD_DIM = 128
MOBA_BLOCK = 256
MOBA_TOPK = 3
PAGE_SIZE = 128
CONV_WIDTH = 31
N_EXPERTS = 32
TOP_K = 4
SWIGLU_ALPHA = 1.702
SWIGLU_LIMIT = 7.0

LANES = 128
CONV_HALO = 32
V7X_VMEM_LIMIT = 56 * 1024 * 1024

_NT = (((1,), (1,)), ((), ()))
_TN = (((0,), (0,)), ((), ()))


def _params(*sem):
    return pltpu.CompilerParams(dimension_semantics=sem, vmem_limit_bytes=V7X_VMEM_LIMIT)


def _cast_rows(src_ref, dst_ref, chunk):
    rows = src_ref.shape[0]
    chunk = min(chunk, rows)

    def body(c, carry):
        r = pl.multiple_of(c * chunk, chunk)
        dst_ref[pl.ds(r, chunk), :] = src_ref[pl.ds(r, chunk), :].astype(dst_ref.dtype)
        return carry

    lax.fori_loop(0, rows // chunk, body, 0)


def _rmsnorm_kernel(x_ref, g_ref, o_ref):
    x = x_ref[...]
    y = x * lax.rsqrt(jnp.mean(x * x, axis=-1, keepdims=True) + EPS)
    o_ref[...] = (y * g_ref[...]).astype(o_ref.dtype)


def rmsnorm_rows(x, g, tm, out_dtype):
    t, d = x.shape
    return pl.pallas_call(
        _rmsnorm_kernel,
        grid=(t // tm,),
        in_specs=[pl.BlockSpec((tm, d), lambda i: (i, 0)), pl.BlockSpec((1, d), lambda i: (0, 0))],
        out_specs=pl.BlockSpec((tm, d), lambda i: (i, 0)),
        out_shape=jax.ShapeDtypeStruct((t, d), out_dtype),
        compiler_params=_params("arbitrary"),
        name="rmsnorm_rows",
    )(x, g.reshape(1, d))


def _proj_kernel(x_ref, *rest, n_w, kind, tn):
    w_refs, rest = rest[:n_w], rest[n_w:]
    if kind in ("headnorm", "residual"):
        e_ref, rest = rest[0], rest[1:]
    o_ref, wb_refs = rest[0], rest[1:]

    @pl.when(pl.program_id(1) == 0)
    def _():
        for w_ref, wb_ref in zip(w_refs, wb_refs):
            _cast_rows(w_ref, wb_ref, 256)

    x = x_ref[...]
    acc = jnp.dot(x, wb_refs[0][...], preferred_element_type=F32)
    if kind == "glu":
        gate = jnp.dot(x, wb_refs[1][...], preferred_element_type=F32)
        o_ref[...] = (acc * jax.nn.sigmoid(gate)).astype(o_ref.dtype)
    elif kind == "headnorm":
        g = e_ref[...]
        for h in range(tn // HEAD_DIM):
            sl = slice(h * HEAD_DIM, (h + 1) * HEAD_DIM)
            blk = acc[:, sl]
            inv = lax.rsqrt(jnp.mean(blk * blk, axis=-1, keepdims=True) + EPS)
            o_ref[:, sl] = (blk * inv * g).astype(o_ref.dtype)
    elif kind == "sigmoid":
        o_ref[...] = jax.nn.sigmoid(acc).astype(o_ref.dtype)
    elif kind == "residual":
        o_ref[...] = (e_ref[...] + acc).astype(o_ref.dtype)
    else:
        o_ref[...] = acc.astype(o_ref.dtype)


def dense_proj(x, w, col0, width, *, kind, out_dtype, tm, tn, extra=None):
    t, k = x.shape
    n_w = 2 if kind == "glu" else 1
    c0 = col0 // tn
    in_specs = [pl.BlockSpec((tm, k), lambda j, i: (i, 0)),
                pl.BlockSpec((k, tn), lambda j, i: (0, c0 + j))]
    args = [x, w]
    if kind == "glu":
        c1 = (col0 + width) // tn
        in_specs.append(pl.BlockSpec((k, tn), lambda j, i: (0, c1 + j)))
        args.append(w)
    if kind == "headnorm":
        in_specs.append(pl.BlockSpec((1, HEAD_DIM), lambda j, i: (0, 0)))
        args.append(extra.reshape(1, HEAD_DIM))
    if kind == "residual":
        in_specs.append(pl.BlockSpec((tm, tn), lambda j, i: (i, j)))
        args.append(extra)
    return pl.pallas_call(
        functools.partial(_proj_kernel, n_w=n_w, kind=kind, tn=tn),
        grid=(width // tn, t // tm),
        in_specs=in_specs,
        out_specs=pl.BlockSpec((tm, tn), lambda j, i: (i, j)),
        out_shape=jax.ShapeDtypeStruct((t, width), out_dtype),
        scratch_shapes=[pltpu.VMEM((k, tn), BF16) for _ in range(n_w)],
        compiler_params=_params("arbitrary", "arbitrary"),
        name="proj_" + kind,
    )(*args)


def _conv_kernel(halo_ref, u_ref, cw_ref, cb_ref, lg_ref, lb_ref, o_ref, ctx_ref, y_ref, *, tt, zero_first_halo,
                 rc, cc):
    if zero_first_halo:
        first = pl.program_id(1) == 0

        @pl.when(first)
        def _():
            ctx_ref[0:CONV_HALO, :] = jnp.zeros((CONV_HALO, ctx_ref.shape[1]), F32)

        @pl.when(jnp.logical_not(first))
        def _():
            ctx_ref[0:CONV_HALO, :] = halo_ref[...]
    else:
        ctx_ref[0:CONV_HALO, :] = halo_ref[...]
    ctx_ref[CONV_HALO:CONV_HALO + tt, :] = u_ref[...]
    d = ctx_ref.shape[1]
    lead = CONV_HALO - (CONV_WIDTH - 1)
    sub = 8

    for c0 in range(0, d, cc):
        def conv_rows(c, carry, c0=c0):
            r = pl.multiple_of(c * rc, rc)
            window = ctx_ref[pl.ds(r, rc + CONV_HALO), c0:c0 + cc]
            acc = jnp.zeros((rc, cc), F32)
            for b in range(sub):
                taps = [s for s in range(lead, lead + CONV_WIDTH) if s % sub == b]
                span = taps[-1] - b + rc
                shifted = window[b:b + span, :]
                for s in taps:
                    w = s - lead
                    acc = acc + shifted[s - b:s - b + rc, :] * cw_ref[w:w + 1, c0:c0 + cc]
            y_ref[pl.ds(r, rc), c0:c0 + cc] = acc + cb_ref[:, c0:c0 + cc]
            return carry

        lax.fori_loop(0, tt // rc, conv_rows, 0)

    lg = lg_ref[...]
    lb = lb_ref[...]

    def norm_rows(c, carry):
        r = pl.multiple_of(c * rc, rc)
        y = y_ref[pl.ds(r, rc), :]
        mu = jnp.mean(y, axis=-1, keepdims=True)
        yc = y - mu
        var = jnp.mean(yc * yc, axis=-1, keepdims=True)
        z = yc * lax.rsqrt(var + EPS) * lg + lb
        o_ref[pl.ds(r, rc), :] = (z * jax.nn.sigmoid(z)).astype(o_ref.dtype)
        return carry

    lax.fori_loop(0, tt // rc, norm_rows, 0)


def conv_branch_act(u, halo, conv_w, conv_b, ln_g, ln_b, *, n_seq, seq_len, tt, out_dtype):
    d = u.shape[1]
    steps = seq_len // tt
    zero_first = halo is None
    if zero_first:
        hb = tt // CONV_HALO
        halo_arr = u
        halo_spec = pl.BlockSpec((CONV_HALO, d), lambda b, t: (jnp.maximum((b * steps + t) * hb - 1, 0), 0))
    else:
        halo_arr = halo
        halo_spec = pl.BlockSpec((CONV_HALO, d), lambda b, t: (b, 0))
    row = lambda b, t: (0, 0)
    return pl.pallas_call(
        functools.partial(_conv_kernel, tt=tt, zero_first_halo=zero_first, rc=16, cc=min(256, d)),
        grid=(n_seq, steps),
        in_specs=[halo_spec,
                  pl.BlockSpec((tt, d), lambda b, t: (b * steps + t, 0)),
                  pl.BlockSpec((CONV_WIDTH, d), row),
                  pl.BlockSpec((1, d), row), pl.BlockSpec((1, d), row), pl.BlockSpec((1, d), row)],
        out_specs=pl.BlockSpec((tt, d), lambda b, t: (b * steps + t, 0)),
        out_shape=jax.ShapeDtypeStruct((n_seq * seq_len, d), out_dtype),
        scratch_shapes=[pltpu.VMEM((CONV_HALO + tt, d), F32), pltpu.VMEM((tt, d), F32)],
        compiler_params=_params("arbitrary", "arbitrary"),
        name="conv_ln_silu",
    )(halo_arr, u, conv_w, conv_b.reshape(1, d), ln_g.reshape(1, d), ln_b.reshape(1, d))


def _moba_prompt_kernel(q_ref, k_ref, v_ref, o_ref, kb_ref, vb_ref, km_ref, *, nb):
    blk = MOBA_BLOCK
    j = pl.program_id(2)

    @pl.when(j == 0)
    def _():
        for n in range(nb):
            kf = k_ref[n * blk:(n + 1) * blk, :]
            kb_ref[n * blk:(n + 1) * blk, :] = kf.astype(BF16)
            vb_ref[n * blk:(n + 1) * blk, :] = v_ref[n * blk:(n + 1) * blk, :].astype(BF16)
            km_ref[n:n + 1, :] = jnp.mean(kf, axis=0, keepdims=True)

    qf = q_ref[...]
    sc = lax.dot_general(km_ref[...], qf, _NT, precision=lax.Precision.HIGHEST, preferred_element_type=F32)
    bi = lax.broadcasted_iota(I32, (nb, blk), 0)
    rank = jnp.zeros((nb, blk), I32)
    for m in range(nb):
        row = sc[m:m + 1, :]
        beats = (row > sc) | ((row == sc) & (m < bi))
        rank = rank + jnp.where(beats, (m < j).astype(I32), 0)
    sel = jnp.where((bi < j) & (rank < MOBA_TOPK), 1.0, 0.0)

    qb = qf.astype(BF16)
    ki = lax.broadcasted_iota(I32, (blk, blk), 0)
    qi = lax.broadcasted_iota(I32, (blk, blk), 1)
    causal = jnp.where(ki <= qi, 1.0, 0.0)
    scale = HEAD_DIM ** -0.5

    def body(n, carry):
        m_run, l_run, acc = carry
        r = pl.multiple_of(n * blk, blk)
        s = lax.dot_general(kb_ref[pl.ds(r, blk), :], qb, _NT, preferred_element_type=F32) * scale
        selrow = jnp.max(jnp.where(bi == n, sel, 0.0), axis=0, keepdims=True)
        own = (n == j).astype(F32)
        mask = (own * causal + (1.0 - own) * selrow) > 0.5
        s = jnp.where(mask, s, NEG)
        m_new = jnp.maximum(m_run, jnp.max(s, axis=0, keepdims=True))
        p = jnp.where(mask, jnp.exp(s - m_new), 0.0)
        alpha = jnp.exp(m_run - m_new)
        l_new = alpha * l_run + jnp.sum(p, axis=0, keepdims=True)
        pv = lax.dot_general(vb_ref[pl.ds(r, blk), :], p.astype(BF16), _TN, preferred_element_type=F32)
        return m_new, l_new, alpha * acc + pv

    init = (jnp.full((1, blk), NEG, F32), jnp.zeros((1, blk), F32), jnp.zeros((HEAD_DIM, blk), F32))
    _, l_run, acc = lax.fori_loop(0, j + 1, body, init)
    o_ref[...] = (acc / l_run).T.astype(o_ref.dtype)


def moba_prompt_attention(q, k, v, *, n_batch, seq_len, out_dtype):
    assert seq_len % MOBA_BLOCK == 0
    nb = seq_len // MOBA_BLOCK
    n_heads = q.shape[1] // HEAD_DIM
    qo_spec = pl.BlockSpec((MOBA_BLOCK, HEAD_DIM), lambda b, h, j: (b * nb + j, h))
    kv_spec = pl.BlockSpec((seq_len, HEAD_DIM), lambda b, h, j: (b, h))
    return pl.pallas_call(
        functools.partial(_moba_prompt_kernel, nb=nb),
        grid=(n_batch, n_heads, nb),
        in_specs=[qo_spec, kv_spec, kv_spec],
        out_specs=qo_spec,
        out_shape=jax.ShapeDtypeStruct((n_batch * seq_len, n_heads * HEAD_DIM), out_dtype),
        scratch_shapes=[pltpu.VMEM((seq_len, HEAD_DIM), BF16), pltpu.VMEM((seq_len, HEAD_DIM), BF16),
                        pltpu.VMEM((nb, HEAD_DIM), F32)],
        compiler_params=_params("arbitrary", "arbitrary", "arbitrary"),
        name="moba_prompt",
    )(q, k, v)


def _moba_sample_kernel(pt_ref, q_ref, ko_ref, vo_ref, k0_ref, k1_ref, v0_ref, v1_ref, o_ref,
                        bias_ref, acc_ref, m_ref, l_ref, sc_ref, *, n_blocks, ds, n_heads):
    b = pl.program_id(0)
    n = pl.program_id(1)
    rows = ds * n_heads
    page_rows = PAGE_SIZE * n_heads
    scale = HEAD_DIM ** -0.5

    @pl.when((b == 0) & (n == 0))
    def _():
        ci = lax.broadcasted_iota(I32, (rows, page_rows), 0)
        ri = lax.broadcasted_iota(I32, (rows, page_rows), 1)
        bias_ref[...] = jnp.where((ci % n_heads) == (ri % n_heads), 0.0, NEG)

    qf = q_ref[0]
    qb = qf.astype(BF16)
    for i, (k_ref, v_ref) in enumerate(((k0_ref, v0_ref), (k1_ref, v1_ref))):
        kf = k_ref[0]
        ksum = jnp.sum(kf.reshape(PAGE_SIZE, n_heads, HEAD_DIM), axis=0)
        ksum_rows = jnp.concatenate([ksum] * ds, axis=0)
        score = jnp.sum(qf * ksum_rows, axis=-1, keepdims=True)
        s = lax.dot_general(qb, kf.astype(BF16), _NT, preferred_element_type=F32) * scale + bias_ref[...]
        m = jnp.max(s, axis=-1, keepdims=True)
        p = jnp.exp(s - m)
        l = jnp.sum(p, axis=-1, keepdims=True)
        acc = jnp.dot(p.astype(BF16), v_ref[0].astype(BF16), preferred_element_type=F32)
        page = 2 * n + i
        acc_ref[page] = acc
        m_ref[page] = jnp.broadcast_to(m, (rows, HEAD_DIM))
        l_ref[page] = jnp.broadcast_to(l, (rows, HEAD_DIM))
        sc_ref[page] = jnp.broadcast_to(score, (rows, HEAD_DIM))

    @pl.when(n == n_blocks - 1)
    def _():
        n_pages = 2 * n_blocks
        shape = (n_blocks, rows, HEAD_DIM)
        scb = jnp.sum(sc_ref[...].reshape(n_blocks, 2, rows, HEAD_DIM), axis=1)
        ni = lax.broadcasted_iota(I32, shape, 0).astype(F32)
        selb = jnp.zeros(shape, F32)
        for _ in range(MOBA_TOPK):
            mx = jnp.max(scb, axis=0, keepdims=True)
            idx = jnp.min(jnp.where(scb == mx, ni, float(n_blocks)), axis=0, keepdims=True)
            hit = ni == idx
            selb = jnp.where(hit, 1.0, selb)
            scb = jnp.where(hit, -jnp.inf, scb)
        selp = jnp.broadcast_to(selb[:, None], (n_blocks, 2, rows, HEAD_DIM)).reshape(n_pages, rows, HEAD_DIM) > 0.5
        mp = m_ref[...]
        top = jnp.max(jnp.where(selp, mp, NEG), axis=0)
        ko = ko_ref[0]
        vo = vo_ref[0]
        q_idx = lax.broadcasted_iota(I32, (rows, HEAD_DIM), 0) // n_heads
        s_own = []
        for t in range(ds):
            kt = jnp.concatenate([ko[t * n_heads:(t + 1) * n_heads]] * ds, axis=0)
            st = jnp.sum(qf * kt, axis=-1, keepdims=True) * scale
            st = jnp.where(q_idx >= t, st, NEG)
            s_own.append(st)
            top = jnp.maximum(top, st)
        w = jnp.where(selp, jnp.exp(mp - top[None]), 0.0)
        den = jnp.sum(w * l_ref[...], axis=0)
        num = jnp.sum(w * acc_ref[...], axis=0)
        for t in range(ds):
            pt = jnp.where(q_idx >= t, jnp.exp(s_own[t] - top), 0.0)
            vt = jnp.concatenate([vo[t * n_heads:(t + 1) * n_heads]] * ds, axis=0)
            den = den + pt
            num = num + pt * vt
        o_ref[0] = num / den


def moba_sample_attention(q, k, v, cache_k, cache_v, page_table):
    db, ds, n_heads, _ = q.shape
    n_pages = page_table.shape[1]
    assert MOBA_BLOCK == 2 * PAGE_SIZE and n_pages % 2 == 0 and ds <= PAGE_SIZE
    n_blocks = n_pages // 2
    assert n_blocks >= MOBA_TOPK
    rows = ds * n_heads
    page_rows = PAGE_SIZE * n_heads
    n_phys = cache_k.shape[0]
    flat = lambda t: t.reshape(db, rows, HEAD_DIM)
    pages = lambda c: c.reshape(n_phys, page_rows, HEAD_DIM)
    row_spec = pl.BlockSpec((1, rows, HEAD_DIM), lambda b, n, pt: (b, 0, 0))

    def page_spec(i):
        return pl.BlockSpec((1, page_rows, HEAD_DIM), lambda b, n, pt: (pt[b * n_pages + 2 * n + i], 0, 0))

    part = pltpu.VMEM((n_pages, rows, HEAD_DIM), F32)
    out = pl.pallas_call(
        functools.partial(_moba_sample_kernel, n_blocks=n_blocks, ds=ds, n_heads=n_heads),
        grid_spec=pltpu.PrefetchScalarGridSpec(
            num_scalar_prefetch=1,
            grid=(db, n_blocks),
            in_specs=[row_spec, row_spec, row_spec, page_spec(0), page_spec(1), page_spec(0), page_spec(1)],
            out_specs=row_spec,
            scratch_shapes=[pltpu.VMEM((rows, page_rows), F32), part, part, part, part]),
        out_shape=jax.ShapeDtypeStruct((db, rows, HEAD_DIM), F32),
        compiler_params=_params("arbitrary", "arbitrary"),
        name="moba_sample",
    )(page_table.reshape(-1), flat(q), flat(k), flat(v), pages(cache_k), pages(cache_k), pages(cache_v),
      pages(cache_v))
    return out.reshape(db * ds, n_heads * HEAD_DIM)


def _merge_kernel(c_ref, a_ref, gc_ref, ga_ref, wc_ref, wa_ref, o_ref, wcb_ref, wab_ref):
    @pl.when(pl.program_id(1) == 0)
    def _():
        _cast_rows(wc_ref, wcb_ref, 256)
        _cast_rows(wa_ref, wab_ref, 256)

    yc = jnp.dot(c_ref[...], wcb_ref[...], preferred_element_type=F32)
    ya = jnp.dot(a_ref[...], wab_ref[...], preferred_element_type=F32)
    o_ref[...] = (gc_ref[...].astype(F32) * yc + ga_ref[...].astype(F32) * ya).astype(o_ref.dtype)


def gated_branch_merge(c_act, attn, sg, w_conv_out, w_attn_out, *, tm, tn, out_dtype):
    t, kc = c_act.shape
    ka = attn.shape[1]
    n = w_conv_out.shape[1]
    nj = n // tn
    return pl.pallas_call(
        _merge_kernel,
        grid=(nj, t // tm),
        in_specs=[pl.BlockSpec((tm, kc), lambda j, i: (i, 0)),
                  pl.BlockSpec((tm, ka), lambda j, i: (i, 0)),
                  pl.BlockSpec((tm, tn), lambda j, i: (i, j)),
                  pl.BlockSpec((tm, tn), lambda j, i: (i, nj + j)),
                  pl.BlockSpec((kc, tn), lambda j, i: (0, j)),
                  pl.BlockSpec((ka, tn), lambda j, i: (0, j))],
        out_specs=pl.BlockSpec((tm, tn), lambda j, i: (i, j)),
        out_shape=jax.ShapeDtypeStruct((t, n), out_dtype),
        scratch_shapes=[pltpu.VMEM((kc, tn), BF16), pltpu.VMEM((ka, tn), BF16)],
        compiler_params=_params("arbitrary", "arbitrary"),
        name="gated_merge",
    )(c_act, attn, sg, sg, w_conv_out, w_attn_out)


def _router_kernel(y_ref, g_ref, wr_ref, br_ref, idx_ref, gate_ref):
    y = y_ref[...]
    hn = y * lax.rsqrt(jnp.mean(y * y, axis=-1, keepdims=True) + EPS) * g_ref[...]
    logits = jnp.dot(hn, wr_ref[...], precision=lax.Precision.HIGHEST, preferred_element_type=F32) + br_ref[...]
    tm, ne = logits.shape
    lane = lax.broadcasted_iota(I32, (tm, ne), 1).astype(F32)
    out_lane = lax.broadcasted_iota(I32, (tm, LANES), 1)
    idx_out = jnp.zeros((tm, LANES), F32)
    val_out = jnp.full((tm, LANES), NEG, F32)
    l = logits
    for s in range(TOP_K):
        m = jnp.max(l, axis=-1, keepdims=True)
        idx = jnp.min(jnp.where(l == m, lane, float(ne)), axis=-1, keepdims=True)
        idx_out = jnp.where(out_lane == s, idx, idx_out)
        val_out = jnp.where(out_lane == s, m, val_out)
        l = jnp.where(lane == idx, -jnp.inf, l)
    top = jnp.max(val_out, axis=-1, keepdims=True)
    e = jnp.where(out_lane < TOP_K, jnp.exp(val_out - top), 0.0)
    idx_ref[...] = idx_out.astype(I32)
    gate_ref[...] = e / jnp.sum(e, axis=-1, keepdims=True)


def moe_router(y, g, w_router, b_router, *, tm):
    t, d = y.shape
    ne = w_router.shape[1]
    row = lambda i: (0, 0)
    return pl.pallas_call(
        _router_kernel,
        grid=(t // tm,),
        in_specs=[pl.BlockSpec((tm, d), lambda i: (i, 0)), pl.BlockSpec((1, d), row),
                  pl.BlockSpec((d, ne), row), pl.BlockSpec((1, ne), row)],
        out_specs=[pl.BlockSpec((tm, LANES), lambda i: (i, 0)),
                   pl.BlockSpec((tm, LANES), lambda i: (i, 0))],
        out_shape=[jax.ShapeDtypeStruct((t, LANES), I32),
                   jax.ShapeDtypeStruct((t, LANES), F32)],
        compiler_params=_params("arbitrary"),
        name="moe_router",
    )(y, g.reshape(1, d), w_router, b_router.reshape(1, ne))


def _row_gather_start(idx_ref, n_rows, src_hbm, buf_ref, sem_ref, slot):
    def body(r, carry):
        row = idx_ref[0, 0, r]
        pltpu.make_async_copy(src_hbm.at[pl.ds(row, 1), :], buf_ref.at[slot, pl.ds(r, 1), :], sem_ref.at[slot]).start()
        return carry

    lax.fori_loop(0, n_rows, body, 0, unroll=8)


def _row_gather_wait(n_rows, src_hbm, buf_ref, sem_ref, slot):
    pltpu.make_async_copy(src_hbm.at[pl.ds(0, n_rows), :], buf_ref.at[slot], sem_ref.at[slot]).wait()


def _dispatch_kernel(cr_ref, idx_ref, idx_next_ref, y_hbm, g_ref, o_ref, buf_ref, sem_ref, *, tg):
    c = pl.program_id(0)
    n_chunks = pl.num_programs(0)
    slot = c % 2
    valid = cr_ref[c] > 0

    @pl.when((c == 0) & valid)
    def _():
        _row_gather_start(idx_ref, tg, y_hbm, buf_ref, sem_ref, 0)

    @pl.when((c + 1 < n_chunks) & (cr_ref[jnp.minimum(c + 1, n_chunks - 1)] > 0))
    def _():
        _row_gather_start(idx_next_ref, tg, y_hbm, buf_ref, sem_ref, 1 - slot)

    @pl.when(valid)
    def _():
        _row_gather_wait(tg, y_hbm, buf_ref, sem_ref, slot)
        y = buf_ref[slot]
        hn = y * lax.rsqrt(jnp.mean(y * y, axis=-1, keepdims=True) + EPS) * g_ref[...]
        o_ref[...] = hn.astype(o_ref.dtype)

    @pl.when(jnp.logical_not(valid))
    def _():
        o_ref[...] = jnp.zeros(o_ref.shape, o_ref.dtype)


def moe_dispatch(y, g, row_tok, chunk_rows, *, tg):
    t, d = y.shape
    n_chunks = row_tok.shape[0] // tg
    idx = row_tok.reshape(n_chunks, 1, tg)
    return pl.pallas_call(
        functools.partial(_dispatch_kernel, tg=tg),
        grid_spec=pltpu.PrefetchScalarGridSpec(
            num_scalar_prefetch=1,
            grid=(n_chunks,),
            in_specs=[pl.BlockSpec((1, 1, tg), lambda c, cr: (c, 0, 0), memory_space=pltpu.SMEM),
                      pl.BlockSpec((1, 1, tg), lambda c, cr: (jnp.minimum(c + 1, n_chunks - 1), 0, 0),
                                   memory_space=pltpu.SMEM),
                      pl.BlockSpec(memory_space=pl.ANY),
                      pl.BlockSpec((1, d), lambda c, cr: (0, 0))],
            out_specs=pl.BlockSpec((tg, d), lambda c, cr: (c, 0)),
            scratch_shapes=[pltpu.VMEM((2, tg, d), F32), pltpu.SemaphoreType.DMA((2,))]),
        out_shape=jax.ShapeDtypeStruct((n_chunks * tg, d), BF16),
        compiler_params=_params("arbitrary"),
        name="moe_dispatch",
    )(chunk_rows, idx, idx, y, g.reshape(1, d))


def _expert_tile_state(te_ref, tr_ref, cnt_ref):
    i = pl.program_id(1)
    nu = cnt_ref[0]
    ic = jnp.minimum(i, nu - 1)
    new_weights = (i == 0) | (te_ref[ic] != te_ref[jnp.maximum(ic - 1, 0)])
    live = i < nu
    return live, live & new_weights, tr_ref[ic], ic


MOE_K_CHUNK = 512


def _for_live_rows(live, recast, n_valid, tm, o_ref, compute):
    half = tm // 2

    @pl.when(jnp.logical_not(live))
    def _():
        o_ref[...] = jnp.zeros(o_ref.shape, o_ref.dtype)

    @pl.when(recast)
    def _():
        compute(0, tm, True)

    @pl.when(live & jnp.logical_not(recast) & (n_valid > half))
    def _():
        compute(0, tm, False)

    @pl.when(live & jnp.logical_not(recast) & (n_valid <= half))
    def _():
        compute(0, half, False)
        o_ref[half:tm, :] = jnp.zeros((tm - half, o_ref.shape[1]), o_ref.dtype)


def _weight_copies(w_hbm, wbuf_ref, sem_ref, e, slot, col_blocks, tn):
    return [pltpu.make_async_copy(w_hbm.at[e, :, pl.ds(pl.multiple_of(cb * tn, tn), tn)],
                                  wbuf_ref.at[slot, m], sem_ref.at[slot])
            for m, cb in enumerate(col_blocks)]


def _stream_expert_weights(te_ref, ne_ref, gi_ref, cnt_ref, ic, recast, w_hbm, wbuf_ref, sem_ref, col_blocks, tn,
                           use_fresh):
    j = pl.program_id(0)
    i = pl.program_id(1)
    nj = pl.num_programs(0)
    slot = (j * cnt_ref[1] + gi_ref[ic]) % 2
    copies = functools.partial(_weight_copies, w_hbm, wbuf_ref, sem_ref)

    @pl.when((j == 0) & (i == 0))
    def _():
        for c in copies(te_ref[0], 0, col_blocks(0), tn):
            c.start()

    @pl.when(recast)
    def _():
        for c in copies(te_ref[ic], slot, col_blocks(j), tn):
            c.wait()
        nxt = ne_ref[ic]

        @pl.when(nxt >= 0)
        def _():
            for c in copies(nxt, 1 - slot, col_blocks(j), tn):
                c.start()

        @pl.when((nxt < 0) & (j + 1 < nj))
        def _():
            for c in copies(te_ref[0], 1 - slot, col_blocks(j + 1), tn):
                c.start()

        use_fresh(slot)


def _dot_casting_weights(x, wbuf_ref, slot, wb_refs):
    k = x.shape[1]
    kc = min(MOE_K_CHUNK, k)
    accs = [None] * len(wb_refs)
    for c0 in range(0, k, kc):
        xk = x[:, c0:c0 + kc]
        for m, wb_ref in enumerate(wb_refs):
            wb = wbuf_ref[slot, m, c0:c0 + kc, :].astype(BF16)
            wb_ref[c0:c0 + kc, :] = wb
            part = jnp.dot(xk, wb, preferred_element_type=F32)
            accs[m] = part if accs[m] is None else accs[m] + part
    return accs


def _moe_up_kernel(te_ref, tr_ref, ne_ref, gi_ref, cnt_ref, x_ref, w_hbm, bg_ref, bl_ref, o_ref,
                   wbuf_ref, sem_ref, wgb_ref, wlb_ref, *, tm, tn):
    live, recast, n_valid, ic = _expert_tile_state(te_ref, tr_ref, cnt_ref)
    nj = pl.num_programs(0)

    def finish(r0, r1, hg, hl):
        x_glu = jnp.minimum(hg + bg_ref[0], SWIGLU_LIMIT)
        x_lin = jnp.clip(hl + bl_ref[0], -SWIGLU_LIMIT, SWIGLU_LIMIT)
        o_ref[r0:r1, :] = (x_glu * jax.nn.sigmoid(SWIGLU_ALPHA * x_glu) * (x_lin + 1.0)).astype(o_ref.dtype)

    def use_fresh(slot):
        hg, hl = _dot_casting_weights(x_ref[...], wbuf_ref, slot, (wgb_ref, wlb_ref))
        finish(0, tm, hg, hl)

    def compute(r0, r1, fresh):
        if fresh:
            _stream_expert_weights(te_ref, ne_ref, gi_ref, cnt_ref, ic, recast, w_hbm, wbuf_ref, sem_ref,
                                   lambda jj: (jj, nj + jj), tn, use_fresh)
            return
        x = x_ref[r0:r1, :]
        finish(r0, r1, jnp.dot(x, wgb_ref[...], preferred_element_type=F32),
               jnp.dot(x, wlb_ref[...], preferred_element_type=F32))

    _for_live_rows(live, recast, n_valid, tm, o_ref, compute)


def _moe_down_kernel(te_ref, tr_ref, ne_ref, gi_ref, cnt_ref, a_ref, w_hbm, b_ref, g_ref, o_ref,
                     wbuf_ref, sem_ref, wb_ref, *, tm, tn):
    live, recast, n_valid, ic = _expert_tile_state(te_ref, tr_ref, cnt_ref)

    def finish(r0, r1, y):
        o_ref[r0:r1, :] = ((y + b_ref[0]) * g_ref[r0:r1, :]).astype(o_ref.dtype)

    def use_fresh(slot):
        (y,) = _dot_casting_weights(a_ref[...], wbuf_ref, slot, (wb_ref,))
        finish(0, tm, y)

    def compute(r0, r1, fresh):
        if fresh:
            _stream_expert_weights(te_ref, ne_ref, gi_ref, cnt_ref, ic, recast, w_hbm, wbuf_ref, sem_ref,
                                   lambda jj: (jj,), tn, use_fresh)
            return
        finish(r0, r1, jnp.dot(a_ref[r0:r1, :], wb_ref[...], preferred_element_type=F32))

    _for_live_rows(live, recast, n_valid, tm, o_ref, compute)


def moe_experts(xs, meta, row_g, w_up, b_up, w_down, b_down, *, tm, tn_up, tn_down):
    r, d = xs.shape
    ne, _, two_ff = w_up.shape
    d_ff = two_ff // 2
    n_tiles = r // tm
    row_i = lambda j, i, te, tr, nx, gi, cnt: (jnp.minimum(i, cnt[0] - 1), 0)
    exp_i = lambda i, te, cnt: te[jnp.minimum(i, cnt[0] - 1)]
    nj = d_ff // tn_up
    hbm = pl.BlockSpec(memory_space=pl.ANY)
    act = pl.pallas_call(
        functools.partial(_moe_up_kernel, tm=tm, tn=tn_up),
        grid_spec=pltpu.PrefetchScalarGridSpec(
            num_scalar_prefetch=5,
            grid=(nj, n_tiles),
            in_specs=[pl.BlockSpec((tm, d), row_i), hbm,
                      pl.BlockSpec((1, 1, tn_up), lambda j, i, te, tr, nx, gi, cnt: (exp_i(i, te, cnt), 0, j)),
                      pl.BlockSpec((1, 1, tn_up), lambda j, i, te, tr, nx, gi, cnt: (exp_i(i, te, cnt), 0, nj + j))],
            out_specs=pl.BlockSpec((tm, tn_up), lambda j, i, te, tr, nx, gi, cnt: (i, j)),
            scratch_shapes=[pltpu.VMEM((2, 2, d, tn_up), F32), pltpu.SemaphoreType.DMA((2,)),
                            pltpu.VMEM((d, tn_up), BF16), pltpu.VMEM((d, tn_up), BF16)]),
        out_shape=jax.ShapeDtypeStruct((r, d_ff), BF16),
        compiler_params=_params("arbitrary", "arbitrary"),
        name="moe_up",
    )(*meta, xs, w_up, b_up.reshape(ne, 1, two_ff), b_up.reshape(ne, 1, two_ff))
    d_out = w_down.shape[2]
    return pl.pallas_call(
        functools.partial(_moe_down_kernel, tm=tm, tn=tn_down),
        grid_spec=pltpu.PrefetchScalarGridSpec(
            num_scalar_prefetch=5,
            grid=(d_out // tn_down, n_tiles),
            in_specs=[pl.BlockSpec((tm, d_ff), row_i), hbm,
                      pl.BlockSpec((1, 1, tn_down), lambda j, i, te, tr, nx, gi, cnt: (exp_i(i, te, cnt), 0, j)),
                      pl.BlockSpec((tm, 1), row_i)],
            out_specs=pl.BlockSpec((tm, tn_down), lambda j, i, te, tr, nx, gi, cnt: (i, j)),
            scratch_shapes=[pltpu.VMEM((2, 1, d_ff, tn_down), F32), pltpu.SemaphoreType.DMA((2,)),
                            pltpu.VMEM((d_ff, tn_down), BF16)]),
        out_shape=jax.ShapeDtypeStruct((r, d_out), F32),
        compiler_params=_params("arbitrary", "arbitrary"),
        name="moe_down",
    )(*meta, act, w_down, b_down.reshape(ne, 1, d_out), row_g.reshape(r, 1))


def _combine_kernel(idx_ref, idx_next_ref, y_ref, yb_hbm, o_ref, buf_ref, sem_ref, *, tq):
    i = pl.program_id(0)
    n_steps = pl.num_programs(0)
    slot = i % 2
    n_rows = TOP_K * tq

    @pl.when(i == 0)
    def _():
        _row_gather_start(idx_ref, n_rows, yb_hbm, buf_ref, sem_ref, 0)

    @pl.when(i + 1 < n_steps)
    def _():
        _row_gather_start(idx_next_ref, n_rows, yb_hbm, buf_ref, sem_ref, 1 - slot)

    _row_gather_wait(n_rows, yb_hbm, buf_ref, sem_ref, slot)
    acc = y_ref[...]
    for s in range(TOP_K):
        acc = acc + buf_ref[slot, s * tq:(s + 1) * tq, :]
    o_ref[...] = acc


def moe_combine(y, yb, dest, *, row0, n_rows, tq):
    d = y.shape[1]
    n_steps = n_rows // tq
    b0 = row0 // tq
    idx = dest[row0:row0 + n_rows].reshape(n_steps, tq, TOP_K).transpose(0, 2, 1).reshape(n_steps, 1, TOP_K * tq)
    return pl.pallas_call(
        functools.partial(_combine_kernel, tq=tq),
        grid=(n_steps,),
        in_specs=[pl.BlockSpec((1, 1, TOP_K * tq), lambda i: (i, 0, 0), memory_space=pltpu.SMEM),
                  pl.BlockSpec((1, 1, TOP_K * tq), lambda i: (jnp.minimum(i + 1, n_steps - 1), 0, 0),
                               memory_space=pltpu.SMEM),
                  pl.BlockSpec((tq, d), lambda i: (b0 + i, 0)),
                  pl.BlockSpec(memory_space=pl.ANY)],
        out_specs=pl.BlockSpec((tq, d), lambda i: (i, 0)),
        out_shape=jax.ShapeDtypeStruct((n_rows, d), F32),
        scratch_shapes=[pltpu.VMEM((2, TOP_K * tq, d), F32), pltpu.SemaphoreType.DMA((2,))],
        compiler_params=_params("arbitrary"),
        name="moe_combine",
    )(idx, idx, y, yb)


def _route_metadata(top_idx, gates, tm, tg):
    t = top_idx.shape[0]
    tk = t * TOP_K
    flat_e = top_idx.reshape(-1)
    onehot = (flat_e[:, None] == jnp.arange(N_EXPERTS, dtype=I32)[None, :]).astype(I32)
    csum = jnp.cumsum(onehot, axis=0)
    rank = jnp.take_along_axis(csum, flat_e[:, None], axis=1)[:, 0] - 1
    counts = csum[-1]
    padded = (counts + tm - 1) // tm * tm
    pends = jnp.cumsum(padded)
    pstarts = pends - padded
    dest = pstarts[flat_e] + rank
    n_tiles = (tk + N_EXPERTS * (tm - 1) + tm - 1) // tm
    rows = n_tiles * tm
    flat_tok = jnp.arange(tk, dtype=I32) // TOP_K
    row_tok = jnp.zeros((rows,), I32).at[dest].set(flat_tok)
    row_g = jnp.zeros((rows,), F32).at[dest].set(gates.reshape(-1))
    tile_start = jnp.arange(n_tiles, dtype=I32) * tm
    tile_e = jnp.minimum(jnp.searchsorted(pends, tile_start, side='right'), N_EXPERTS - 1).astype(I32)
    tile_rows = jnp.clip(counts[tile_e] - (tile_start - pstarts[tile_e]), 0, tm).astype(I32)
    per = tm // tg
    chunk_rows = jnp.clip(jnp.repeat(tile_rows, per) - jnp.tile(jnp.arange(per, dtype=I32) * tg, n_tiles), 0, tg)
    has_rows = counts > 0
    ids = jnp.arange(N_EXPERTS, dtype=I32)
    later = jnp.where(has_rows[None, :] & (ids[None, :] > ids[:, None]), ids[None, :], N_EXPERTS)
    next_e = jnp.min(later, axis=1)
    next_e = jnp.where(next_e < N_EXPERTS, next_e, -1).astype(I32)
    group_i = (jnp.cumsum(has_rows.astype(I32)) - 1).astype(I32)
    cnt = jnp.stack([pends[-1] // tm, jnp.sum(has_rows.astype(I32))]).astype(I32)
    meta = (tile_e, tile_rows, next_e[tile_e], group_i[tile_e], cnt)
    return dest.reshape(t, TOP_K), row_tok, row_g, meta, chunk_rows.astype(I32)


TILES = dict(tm_light=320, tm=640, tn=512, tn_glu=256, tt_conv=256, tm_e=512, tg=256, tq=128, tn_up=512,
             tn_down=512)


def kernel(x_prompt, x_sample, cache_k, cache_v, state_conv, page_table, attn_norm_g, w_in, conv_w, conv_b,
           conv_ln_g, conv_ln_b, w_conv_out, q_norm_g, k_norm_g, w_attn_out, w_o, ffn_norm_g, w_router,
           b_router, w_up, b_up, w_down, b_down):
    return _step(TILES, x_prompt, x_sample, cache_k, cache_v, state_conv, page_table, attn_norm_g, w_in, conv_w,
                 conv_b, conv_ln_g, conv_ln_b, w_conv_out, q_norm_g, k_norm_g, w_attn_out, w_o, ffn_norm_g,
                 w_router, b_router, w_up, b_up, w_down, b_down)


def _step(cfg, x_prompt, x_sample, cache_k, cache_v, state_conv, page_table, attn_norm_g, w_in, conv_w, conv_b,
          conv_ln_g, conv_ln_b, w_conv_out, q_norm_g, k_norm_g, w_attn_out, w_o, ffn_norm_g, w_router,
          b_router, w_up, b_up, w_down, b_down):
    n_batch, seq_len, d_model = x_prompt.shape
    db, ds, _ = x_sample.shape
    d_conv = conv_w.shape[1]
    d_attn = w_attn_out.shape[0]
    n_heads = d_attn // HEAD_DIM
    past_len = page_table.shape[1] * PAGE_SIZE
    tp = n_batch * seq_len
    ts = db * ds

    x = jnp.concatenate([x_prompt.reshape(tp, d_model), x_sample.reshape(ts, d_model)], axis=0)
    n = rmsnorm_rows(x, attn_norm_g, cfg["tm_light"], BF16)

    proj = functools.partial(dense_proj, n, w_in, tm=cfg["tm"])
    u = proj(0, d_conv, kind="glu", out_dtype=F32, tn=cfg["tn_glu"])
    c = 2 * d_conv
    q = proj(c, d_attn, kind="headnorm", out_dtype=F32, tn=cfg["tn"], extra=q_norm_g)
    k = proj(c + d_attn, d_attn, kind="headnorm", out_dtype=F32, tn=cfg["tn"], extra=k_norm_g)
    v = proj(c + 2 * d_attn, d_attn, kind="none", out_dtype=F32, tn=cfg["tn"])
    sg = proj(c + 3 * d_attn, 2 * d_model, kind="sigmoid", out_dtype=BF16, tn=cfg["tn"])

    conv = functools.partial(conv_branch_act, conv_w=conv_w, conv_b=conv_b, ln_g=conv_ln_g, ln_b=conv_ln_b,
                             out_dtype=BF16)
    c_prompt = conv(u, None, n_seq=n_batch, seq_len=seq_len, tt=cfg["tt_conv"])
    u_s = u[tp:].reshape(db, ds, d_conv)
    ds_pad = -(-ds // 16) * 16
    lead = CONV_HALO - (CONV_WIDTH - 1)
    halo_s = jnp.pad(state_conv.astype(F32), ((0, 0), (lead, 0), (0, 0))).reshape(db * CONV_HALO, d_conv)
    u_s_pad = jnp.pad(u_s, ((0, 0), (0, ds_pad - ds), (0, 0))).reshape(db * ds_pad, d_conv)
    c_sample = conv(u_s_pad, halo_s, n_seq=db, seq_len=ds_pad, tt=ds_pad)
    c_sample = c_sample.reshape(db, ds_pad, d_conv)[:, :ds].reshape(ts, d_conv)
    c_act = jnp.concatenate([c_prompt[:tp], c_sample], axis=0)

    a_prompt = moba_prompt_attention(q, k, v, n_batch=n_batch, seq_len=seq_len, out_dtype=BF16)
    heads = lambda t_: t_[tp:].reshape(db, ds, n_heads, HEAD_DIM)
    a_sample = moba_sample_attention(heads(q), heads(k), heads(v), cache_k, cache_v, page_table)
    attn = jnp.concatenate([a_prompt, a_sample.astype(BF16)], axis=0)

    merged = gated_branch_merge(c_act, attn, sg, w_conv_out, w_attn_out, tm=cfg["tm"], tn=cfg["tn"], out_dtype=BF16)
    y1 = dense_proj(merged, w_o, 0, d_model, kind="residual", out_dtype=F32, tm=cfg["tm"], tn=cfg["tn"], extra=x)

    idx_pad, gate_pad = moe_router(y1, ffn_norm_g, w_router, b_router, tm=cfg["tm_light"])
    dest, row_tok, row_g, meta, chunk_rows = _route_metadata(
        idx_pad[:, :TOP_K], gate_pad[:, :TOP_K], cfg["tm_e"], cfg["tg"])
    xs = moe_dispatch(y1, ffn_norm_g, row_tok, chunk_rows, tg=cfg["tg"])
    yb = moe_experts(xs, meta, row_g, w_up, b_up, w_down, b_down,
                     tm=cfg["tm_e"], tn_up=cfg["tn_up"], tn_down=cfg["tn_down"])
    y_p = moe_combine(y1, yb, dest, row0=0, n_rows=tp, tq=cfg["tq"])
    y_s = moe_combine(y1, yb, dest, row0=tp, n_rows=ts, tq=cfg["tq"])

    u_p = u[:tp].reshape(n_batch, seq_len, d_conv)
    conv_prompt = jnp.pad(u_p, ((0, 0), (CONV_WIDTH - 1, 0), (0, 0)))[:, -(CONV_WIDTH - 1):]
    conv_sample = jnp.concatenate([state_conv.astype(F32), u_s], axis=1)[:, -(CONV_WIDTH - 1):]
    hp = lambda t_: t_[:tp].reshape(n_batch, seq_len, n_heads, HEAD_DIM)
    return (y_p.reshape(n_batch, seq_len, d_model), y_s.reshape(db, ds, d_model),
            hp(k), hp(v), conv_prompt, heads(k), heads(v), conv_sample)
```

```python
import functools

import jax
import jax.numpy as jnp
from jax import lax
from jax.experimental import pallas as pl
from jax.experimental.pallas import tpu as pltpu

F32 = jnp.float32
BF16 = jnp.bfloat16
I32 = jnp.int32

EPS = 1e-6
NEG = -1e30
HEAD_DIM = 128
MOBA_BLOCK = 256
MOBA_TOPK = 3
PAGE_SIZE = 128
CONV_WIDTH = 31
N_EXPERTS = 32
TOP_K = 4
SWIGLU_ALPHA = 1.702
SWIGLU_LIMIT = 7.0

LANES = 128
CONV_HALO = 32
V7X_VMEM_LIMIT = 56 * 1024 * 1024

_NT = (((1,), (1,)), ((), ()))
_TN = (((0,), (0,)), ((), ()))


def _params(*sem):
    return pltpu.CompilerParams(dimension_semantics=sem, vmem_limit_bytes=V7X_VMEM_LIMIT)


def _cast_rows(src_ref, dst_ref, chunk):
    rows = src_ref.shape[0]
    chunk = min(chunk, rows)

    def body(c, carry):
        r = pl.multiple_of(c * chunk, chunk)
        dst_ref[pl.ds(r, chunk), :] = src_ref[pl.ds(r, chunk), :].astype(dst_ref.dtype)
        return carry

    lax.fori_loop(0, rows // chunk, body, 0)


def _rmsnorm_kernel(x_ref, g_ref, o_ref):
    x = x_ref[...]
    y = x * lax.rsqrt(jnp.mean(x * x, axis=-1, keepdims=True) + EPS)
    o_ref[...] = (y * g_ref[...]).astype(o_ref.dtype)


def rmsnorm_rows(x, g, tm, out_dtype):
    t, d = x.shape
    return pl.pallas_call(
        _rmsnorm_kernel,
        grid=(t // tm,),
        in_specs=[pl.BlockSpec((tm, d), lambda i: (i, 0)), pl.BlockSpec((1, d), lambda i: (0, 0))],
        out_specs=pl.BlockSpec((tm, d), lambda i: (i, 0)),
        out_shape=jax.ShapeDtypeStruct((t, d), out_dtype),
        compiler_params=_params("arbitrary"),
        name="rmsnorm_rows",
    )(x, g.reshape(1, d))


def _proj_kernel(x_ref, *rest, n_w, kind, tn):
    w_refs, rest = rest[:n_w], rest[n_w:]
    if kind in ("headnorm", "residual"):
        e_ref, rest = rest[0], rest[1:]
    o_ref, wb_refs = rest[0], rest[1:]

    @pl.when(pl.program_id(1) == 0)
    def _():
        for w_ref, wb_ref in zip(w_refs, wb_refs):
            _cast_rows(w_ref, wb_ref, 256)

    x = x_ref[...]
    acc = jnp.dot(x, wb_refs[0][...], preferred_element_type=F32)
    if kind == "glu":
        gate = jnp.dot(x, wb_refs[1][...], preferred_element_type=F32)
        o_ref[...] = (acc * jax.nn.sigmoid(gate)).astype(o_ref.dtype)
    elif kind == "headnorm":
        g = e_ref[...]
        for h in range(tn // HEAD_DIM):
            sl = slice(h * HEAD_DIM, (h + 1) * HEAD_DIM)
            blk = acc[:, sl]
            inv = lax.rsqrt(jnp.mean(blk * blk, axis=-1, keepdims=True) + EPS)
            o_ref[:, sl] = (blk * inv * g).astype(o_ref.dtype)
    elif kind == "sigmoid":
        o_ref[...] = jax.nn.sigmoid(acc).astype(o_ref.dtype)
    elif kind == "residual":
        o_ref[...] = (e_ref[...] + acc).astype(o_ref.dtype)
    else:
        o_ref[...] = acc.astype(o_ref.dtype)


def dense_proj(x, w, col0, width, *, kind, out_dtype, tm, tn, extra=None):
    t, k = x.shape
    n_w = 2 if kind == "glu" else 1
    c0 = col0 // tn
    in_specs = [pl.BlockSpec((tm, k), lambda j, i: (i, 0)),
                pl.BlockSpec((k, tn), lambda j, i: (0, c0 + j))]
    args = [x, w]
    if kind == "glu":
        c1 = (col0 + width) // tn
        in_specs.append(pl.BlockSpec((k, tn), lambda j, i: (0, c1 + j)))
        args.append(w)
    if kind == "headnorm":
        in_specs.append(pl.BlockSpec((1, HEAD_DIM), lambda j, i: (0, 0)))
        args.append(extra.reshape(1, HEAD_DIM))
    if kind == "residual":
        in_specs.append(pl.BlockSpec((tm, tn), lambda j, i: (i, j)))
        args.append(extra)
    return pl.pallas_call(
        functools.partial(_proj_kernel, n_w=n_w, kind=kind, tn=tn),
        grid=(width // tn, t // tm),
        in_specs=in_specs,
        out_specs=pl.BlockSpec((tm, tn), lambda j, i: (i, j)),
        out_shape=jax.ShapeDtypeStruct((t, width), out_dtype),
        scratch_shapes=[pltpu.VMEM((k, tn), BF16) for _ in range(n_w)],
        compiler_params=_params("arbitrary", "arbitrary"),
        name="proj_" + kind,
    )(*args)


def _conv_kernel(halo_ref, u_ref, cw_ref, cb_ref, lg_ref, lb_ref, o_ref, ctx_ref, y_ref, *, tt, zero_first_halo,
                 rc, cc):
    if zero_first_halo:
        first = pl.program_id(1) == 0

        @pl.when(first)
        def _():
            ctx_ref[0:CONV_HALO, :] = jnp.zeros((CONV_HALO, ctx_ref.shape[1]), F32)

        @pl.when(jnp.logical_not(first))
        def _():
            ctx_ref[0:CONV_HALO, :] = halo_ref[...]
    else:
        ctx_ref[0:CONV_HALO, :] = halo_ref[...]
    ctx_ref[CONV_HALO:CONV_HALO + tt, :] = u_ref[...]
    d = ctx_ref.shape[1]
    lead = CONV_HALO - (CONV_WIDTH - 1)
    sub = 8

    for c0 in range(0, d, cc):
        def conv_rows(c, carry, c0=c0):
            r = pl.multiple_of(c * rc, rc)
            window = ctx_ref[pl.ds(r, rc + CONV_HALO), c0:c0 + cc]
            acc = jnp.zeros((rc, cc), F32)
            for b in range(sub):
                taps = [s for s in range(lead, lead + CONV_WIDTH) if s % sub == b]
                span = taps[-1] - b + rc
                shifted = window[b:b + span, :]
                for s in taps:
                    w = s - lead
                    acc = acc + shifted[s - b:s - b + rc, :] * cw_ref[w:w + 1, c0:c0 + cc]
            y_ref[pl.ds(r, rc), c0:c0 + cc] = acc + cb_ref[:, c0:c0 + cc]
            return carry

        lax.fori_loop(0, tt // rc, conv_rows, 0)

    lg = lg_ref[...]
    lb = lb_ref[...]

    def norm_rows(c, carry):
        r = pl.multiple_of(c * rc, rc)
        y = y_ref[pl.ds(r, rc), :]
        mu = jnp.mean(y, axis=-1, keepdims=True)
        yc = y - mu
        var = jnp.mean(yc * yc, axis=-1, keepdims=True)
        z = yc * lax.rsqrt(var + EPS) * lg + lb
        o_ref[pl.ds(r, rc), :] = (z * jax.nn.sigmoid(z)).astype(o_ref.dtype)
        return carry

    lax.fori_loop(0, tt // rc, norm_rows, 0)


def conv_branch_act(u, halo, conv_w, conv_b, ln_g, ln_b, *, n_seq, seq_len, tt, out_dtype):
    d = u.shape[1]
    steps = seq_len // tt
    zero_first = halo is None
    if zero_first:
        hb = tt // CONV_HALO
        halo_arr = u
        halo_spec = pl.BlockSpec((CONV_HALO, d), lambda b, t: (jnp.maximum((b * steps + t) * hb - 1, 0), 0))
    else:
        halo_arr = halo
        halo_spec = pl.BlockSpec((CONV_HALO, d), lambda b, t: (b, 0))
    row = lambda b, t: (0, 0)
    return pl.pallas_call(
        functools.partial(_conv_kernel, tt=tt, zero_first_halo=zero_first, rc=16, cc=min(256, d)),
        grid=(n_seq, steps),
        in_specs=[halo_spec,
                  pl.BlockSpec((tt, d), lambda b, t: (b * steps + t, 0)),
                  pl.BlockSpec((CONV_WIDTH, d), row),
                  pl.BlockSpec((1, d), row), pl.BlockSpec((1, d), row), pl.BlockSpec((1, d), row)],
        out_specs=pl.BlockSpec((tt, d), lambda b, t: (b * steps + t, 0)),
        out_shape=jax.ShapeDtypeStruct((n_seq * seq_len, d), out_dtype),
        scratch_shapes=[pltpu.VMEM((CONV_HALO + tt, d), F32), pltpu.VMEM((tt, d), F32)],
        compiler_params=_params("arbitrary", "arbitrary"),
        name="conv_ln_silu",
    )(halo_arr, u, conv_w, conv_b.reshape(1, d), ln_g.reshape(1, d), ln_b.reshape(1, d))


def _moba_prompt_kernel(q_ref, k_ref, v_ref, o_ref, kb_ref, vb_ref, km_ref, *, nb, hp):
    blk = MOBA_BLOCK
    j = pl.program_id(2)

    @pl.when(j == 0)
    def _():
        for n in range(nb):
            kf = k_ref[n * blk:(n + 1) * blk, :]
            kb_ref[n * blk:(n + 1) * blk, :] = kf.astype(BF16)
            vb_ref[n * blk:(n + 1) * blk, :] = v_ref[n * blk:(n + 1) * blk, :].astype(BF16)
            km_ref[n:n + 1, :] = jnp.mean(kf, axis=0, keepdims=True)

    bi = lax.broadcasted_iota(I32, (nb, blk), 0)
    ki = lax.broadcasted_iota(I32, (blk, blk), 0)
    qi = lax.broadcasted_iota(I32, (blk, blk), 1)
    causal = jnp.where(ki <= qi, 1.0, 0.0)
    scale = HEAD_DIM ** -0.5
    sels, qbs = [], []
    for h in range(hp):
        hs = slice(h * HEAD_DIM, (h + 1) * HEAD_DIM)
        qf = q_ref[:, hs]
        sc = lax.dot_general(km_ref[:, hs], qf, _NT, precision=lax.Precision.HIGHEST, preferred_element_type=F32)
        rank = jnp.zeros((nb, blk), I32)
        for m in range(nb):
            row = sc[m:m + 1, :]
            beats = (row > sc) | ((row == sc) & (m < bi))
            rank = rank + jnp.where(beats, (m < j).astype(I32), 0)
        sels.append(jnp.where((bi < j) & (rank < MOBA_TOPK), 1.0, 0.0))
        qbs.append(qf.astype(BF16))

    def body(n, carry):
        r = pl.multiple_of(n * blk, blk)
        own = (n == j).astype(F32)
        out = []
        for h in range(hp):
            hs = slice(h * HEAD_DIM, (h + 1) * HEAD_DIM)
            m_run, l_run, acc = carry[h]
            s = lax.dot_general(kb_ref[pl.ds(r, blk), hs], qbs[h], _NT, preferred_element_type=F32) * scale
            selrow = jnp.max(jnp.where(bi == n, sels[h], 0.0), axis=0, keepdims=True)
            mask = (own * causal + (1.0 - own) * selrow) > 0.5
            s = jnp.where(mask, s, NEG)
            m_new = jnp.maximum(m_run, jnp.max(s, axis=0, keepdims=True))
            p = jnp.where(mask, jnp.exp(s - m_new), 0.0)
            alpha = jnp.exp(m_run - m_new)
            l_new = alpha * l_run + jnp.sum(p, axis=0, keepdims=True)
            pv = lax.dot_general(vb_ref[pl.ds(r, blk), hs], p.astype(BF16), _TN, preferred_element_type=F32)
            out.append((m_new, l_new, alpha * acc + pv))
        return tuple(out)

    init = tuple((jnp.full((1, blk), NEG, F32), jnp.zeros((1, blk), F32), jnp.zeros((HEAD_DIM, blk), F32))
                 for _ in range(hp))
    res = lax.fori_loop(0, j + 1, body, init)
    for h in range(hp):
        _, l_run, acc = res[h]
        o_ref[:, h * HEAD_DIM:(h + 1) * HEAD_DIM] = (acc / l_run).T.astype(o_ref.dtype)


def moba_prompt_attention(q, k, v, *, n_batch, seq_len, out_dtype, hp=2):
    assert seq_len % MOBA_BLOCK == 0
    nb = seq_len // MOBA_BLOCK
    n_heads = q.shape[1] // HEAD_DIM
    assert n_heads % hp == 0
    w = hp * HEAD_DIM
    qo_spec = pl.BlockSpec((MOBA_BLOCK, w), lambda b, h, j: (b * nb + j, h))
    kv_spec = pl.BlockSpec((seq_len, w), lambda b, h, j: (b, h))
    return pl.pallas_call(
        functools.partial(_moba_prompt_kernel, nb=nb, hp=hp),
        grid=(n_batch, n_heads // hp, nb),
        in_specs=[qo_spec, kv_spec, kv_spec],
        out_specs=qo_spec,
        out_shape=jax.ShapeDtypeStruct((n_batch * seq_len, n_heads * HEAD_DIM), out_dtype),
        scratch_shapes=[pltpu.VMEM((seq_len, w), BF16), pltpu.VMEM((seq_len, w), BF16),
                        pltpu.VMEM((nb, w), F32)],
        compiler_params=_params("arbitrary", "arbitrary", "arbitrary"),
        name="moba_prompt",
    )(q, k, v)


def _moba_sample_kernel(pt_ref, q_ref, ko_ref, vo_ref, k0_ref, k1_ref, v0_ref, v1_ref, o_ref,
                        bias_ref, acc_ref, m_ref, l_ref, sc_ref, *, n_blocks, ds, n_heads):
    b = pl.program_id(0)
    n = pl.program_id(1)
    rows = ds * n_heads
    page_rows = PAGE_SIZE * n_heads
    scale = HEAD_DIM ** -0.5

    @pl.when((b == 0) & (n == 0))
    def _():
        ci = lax.broadcasted_iota(I32, (rows, page_rows), 0)
        ri = lax.broadcasted_iota(I32, (rows, page_rows), 1)
        bias_ref[...] = jnp.where((ci % n_heads) == (ri % n_heads), 0.0, NEG)

    qf = q_ref[0]
    qb = qf.astype(BF16)
    for i, (k_ref, v_ref) in enumerate(((k0_ref, v0_ref), (k1_ref, v1_ref))):
        kf = k_ref[0]
        ksum = jnp.sum(kf.reshape(PAGE_SIZE, n_heads, HEAD_DIM), axis=0)
        ksum_rows = jnp.concatenate([ksum] * ds, axis=0)
        score = jnp.sum(qf * ksum_rows, axis=-1, keepdims=True)
        s = lax.dot_general(qb, kf.astype(BF16), _NT, preferred_element_type=F32) * scale + bias_ref[...]
        m = jnp.max(s, axis=-1, keepdims=True)
        p = jnp.exp(s - m)
        l = jnp.sum(p, axis=-1, keepdims=True)
        acc = jnp.dot(p.astype(BF16), v_ref[0].astype(BF16), preferred_element_type=F32)
        page = 2 * n + i
        acc_ref[page] = acc
        m_ref[page] = jnp.broadcast_to(m, (rows, HEAD_DIM))
        l_ref[page] = jnp.broadcast_to(l, (rows, HEAD_DIM))
        sc_ref[page] = jnp.broadcast_to(score, (rows, HEAD_DIM))

    @pl.when(n == n_blocks - 1)
    def _():
        n_pages = 2 * n_blocks
        shape = (n_blocks, rows, HEAD_DIM)
        scb = jnp.sum(sc_ref[...].reshape(n_blocks, 2, rows, HEAD_DIM), axis=1)
        ni = lax.broadcasted_iota(I32, shape, 0).astype(F32)
        selb = jnp.zeros(shape, F32)
        for _ in range(MOBA_TOPK):
            mx = jnp.max(scb, axis=0, keepdims=True)
            idx = jnp.min(jnp.where(scb == mx, ni, float(n_blocks)), axis=0, keepdims=True)
            hit = ni == idx
            selb = jnp.where(hit, 1.0, selb)
            scb = jnp.where(hit, -jnp.inf, scb)
        selp = jnp.broadcast_to(selb[:, None], (n_blocks, 2, rows, HEAD_DIM)).reshape(n_pages, rows, HEAD_DIM) > 0.5
        mp = m_ref[...]
        top = jnp.max(jnp.where(selp, mp, NEG), axis=0)
        ko = ko_ref[0]
        vo = vo_ref[0]
        q_idx = lax.broadcasted_iota(I32, (rows, HEAD_DIM), 0) // n_heads
        s_own = []
        for t in range(ds):
            kt = jnp.concatenate([ko[t * n_heads:(t + 1) * n_heads]] * ds, axis=0)
            st = jnp.sum(qf * kt, axis=-1, keepdims=True) * scale
            st = jnp.where(q_idx >= t, st, NEG)
            s_own.append(st)
            top = jnp.maximum(top, st)
        w = jnp.where(selp, jnp.exp(mp - top[None]), 0.0)
        den = jnp.sum(w * l_ref[...], axis=0)
        num = jnp.sum(w * acc_ref[...], axis=0)
        for t in range(ds):
            pt = jnp.where(q_idx >= t, jnp.exp(s_own[t] - top), 0.0)
            vt = jnp.concatenate([vo[t * n_heads:(t + 1) * n_heads]] * ds, axis=0)
            den = den + pt
            num = num + pt * vt
        o_ref[0] = num / den


def moba_sample_attention(q, k, v, cache_k, cache_v, page_table):
    db, ds, n_heads, _ = q.shape
    n_pages = page_table.shape[1]
    assert MOBA_BLOCK == 2 * PAGE_SIZE and n_pages % 2 == 0 and ds <= PAGE_SIZE
    n_blocks = n_pages // 2
    assert n_blocks >= MOBA_TOPK
    rows = ds * n_heads
    page_rows = PAGE_SIZE * n_heads
    n_phys = cache_k.shape[0]
    flat = lambda t: t.reshape(db, rows, HEAD_DIM)
    pages = lambda c: c.reshape(n_phys, page_rows, HEAD_DIM)
    row_spec = pl.BlockSpec((1, rows, HEAD_DIM), lambda b, n, pt: (b, 0, 0))

    def page_spec(i):
        return pl.BlockSpec((1, page_rows, HEAD_DIM), lambda b, n, pt: (pt[b * n_pages + 2 * n + i], 0, 0))

    part = pltpu.VMEM((n_pages, rows, HEAD_DIM), F32)
    out = pl.pallas_call(
        functools.partial(_moba_sample_kernel, n_blocks=n_blocks, ds=ds, n_heads=n_heads),
        grid_spec=pltpu.PrefetchScalarGridSpec(
            num_scalar_prefetch=1,
            grid=(db, n_blocks),
            in_specs=[row_spec, row_spec, row_spec, page_spec(0), page_spec(1), page_spec(0), page_spec(1)],
            out_specs=row_spec,
            scratch_shapes=[pltpu.VMEM((rows, page_rows), F32), part, part, part, part]),
        out_shape=jax.ShapeDtypeStruct((db, rows, HEAD_DIM), F32),
        compiler_params=_params("arbitrary", "arbitrary"),
        name="moba_sample",
    )(page_table.reshape(-1), flat(q), flat(k), flat(v), pages(cache_k), pages(cache_k), pages(cache_v),
      pages(cache_v))
    return out.reshape(db * ds, n_heads * HEAD_DIM)


def _merge_kernel(c_ref, a_ref, gc_ref, ga_ref, wc_ref, wa_ref, o_ref, wcb_ref, wab_ref):
    @pl.when(pl.program_id(1) == 0)
    def _():
        _cast_rows(wc_ref, wcb_ref, 256)
        _cast_rows(wa_ref, wab_ref, 256)

    yc = jnp.dot(c_ref[...], wcb_ref[...], preferred_element_type=F32)
    ya = jnp.dot(a_ref[...], wab_ref[...], preferred_element_type=F32)
    o_ref[...] = (gc_ref[...].astype(F32) * yc + ga_ref[...].astype(F32) * ya).astype(o_ref.dtype)


def gated_branch_merge(c_act, attn, sg, w_conv_out, w_attn_out, *, tm, tn, out_dtype):
    t, kc = c_act.shape
    ka = attn.shape[1]
    n = w_conv_out.shape[1]
    nj = n // tn
    return pl.pallas_call(
        _merge_kernel,
        grid=(nj, t // tm),
        in_specs=[pl.BlockSpec((tm, kc), lambda j, i: (i, 0)),
                  pl.BlockSpec((tm, ka), lambda j, i: (i, 0)),
                  pl.BlockSpec((tm, tn), lambda j, i: (i, j)),
                  pl.BlockSpec((tm, tn), lambda j, i: (i, nj + j)),
                  pl.BlockSpec((kc, tn), lambda j, i: (0, j)),
                  pl.BlockSpec((ka, tn), lambda j, i: (0, j))],
        out_specs=pl.BlockSpec((tm, tn), lambda j, i: (i, j)),
        out_shape=jax.ShapeDtypeStruct((t, n), out_dtype),
        scratch_shapes=[pltpu.VMEM((kc, tn), BF16), pltpu.VMEM((ka, tn), BF16)],
        compiler_params=_params("arbitrary", "arbitrary"),
        name="gated_merge",
    )(c_act, attn, sg, sg, w_conv_out, w_attn_out)


def _router_kernel(y_ref, g_ref, wr_ref, br_ref, idx_ref, gate_ref):
    y = y_ref[...]
    hn = y * lax.rsqrt(jnp.mean(y * y, axis=-1, keepdims=True) + EPS) * g_ref[...]
    logits = jnp.dot(hn, wr_ref[...], precision=lax.Precision.HIGHEST, preferred_element_type=F32) + br_ref[...]
    tm, ne = logits.shape
    lane = lax.broadcasted_iota(I32, (tm, ne), 1).astype(F32)
    out_lane = lax.broadcasted_iota(I32, (tm, LANES), 1)
    idx_out = jnp.zeros((tm, LANES), F32)
    val_out = jnp.full((tm, LANES), NEG, F32)
    l = logits
    for s in range(TOP_K):
        m = jnp.max(l, axis=-1, keepdims=True)
        idx = jnp.min(jnp.where(l == m, lane, float(ne)), axis=-1, keepdims=True)
        idx_out = jnp.where(out_lane == s, idx, idx_out)
        val_out = jnp.where(out_lane == s, m, val_out)
        l = jnp.where(lane == idx, -jnp.inf, l)
    top = jnp.max(val_out, axis=-1, keepdims=True)
    e = jnp.where(out_lane < TOP_K, jnp.exp(val_out - top), 0.0)
    idx_ref[...] = idx_out.astype(I32)
    gate_ref[...] = e / jnp.sum(e, axis=-1, keepdims=True)


def moe_router(y, g, w_router, b_router, *, tm):
    t, d = y.shape
    ne = w_router.shape[1]
    row = lambda i: (0, 0)
    return pl.pallas_call(
        _router_kernel,
        grid=(t // tm,),
        in_specs=[pl.BlockSpec((tm, d), lambda i: (i, 0)), pl.BlockSpec((1, d), row),
                  pl.BlockSpec((d, ne), row), pl.BlockSpec((1, ne), row)],
        out_specs=[pl.BlockSpec((tm, LANES), lambda i: (i, 0)),
                   pl.BlockSpec((tm, LANES), lambda i: (i, 0))],
        out_shape=[jax.ShapeDtypeStruct((t, LANES), I32),
                   jax.ShapeDtypeStruct((t, LANES), F32)],
        compiler_params=_params("arbitrary"),
        name="moe_router",
    )(y, g.reshape(1, d), w_router, b_router.reshape(1, ne))


def _row_gather_start(idx_ref, n_rows, src_hbm, buf_ref, sem_ref, slot):
    def body(r, carry):
        row = idx_ref[0, 0, r]
        pltpu.make_async_copy(src_hbm.at[pl.ds(row, 1), :], buf_ref.at[slot, pl.ds(r, 1), :], sem_ref.at[slot]).start()
        return carry

    lax.fori_loop(0, n_rows, body, 0, unroll=8)


def _row_gather_wait(n_rows, src_hbm, buf_ref, sem_ref, slot):
    pltpu.make_async_copy(src_hbm.at[pl.ds(0, n_rows), :], buf_ref.at[slot], sem_ref.at[slot]).wait()


def _dispatch_kernel(cr_ref, idx_ref, idx_next_ref, y_hbm, g_ref, o_ref, buf_ref, sem_ref, *, tg):
    c = pl.program_id(0)
    n_chunks = pl.num_programs(0)
    slot = c % 2
    valid = cr_ref[c] > 0

    @pl.when((c == 0) & valid)
    def _():
        _row_gather_start(idx_ref, tg, y_hbm, buf_ref, sem_ref, 0)

    @pl.when((c + 1 < n_chunks) & (cr_ref[jnp.minimum(c + 1, n_chunks - 1)] > 0))
    def _():
        _row_gather_start(idx_next_ref, tg, y_hbm, buf_ref, sem_ref, 1 - slot)

    @pl.when(valid)
    def _():
        _row_gather_wait(tg, y_hbm, buf_ref, sem_ref, slot)
        y = buf_ref[slot]
        hn = y * lax.rsqrt(jnp.mean(y * y, axis=-1, keepdims=True) + EPS) * g_ref[...]
        o_ref[...] = hn.astype(o_ref.dtype)

    @pl.when(jnp.logical_not(valid))
    def _():
        o_ref[...] = jnp.zeros(o_ref.shape, o_ref.dtype)


def moe_dispatch(y, g, row_tok, chunk_rows, *, tg):
    t, d = y.shape
    n_chunks = row_tok.shape[0] // tg
    idx = row_tok.reshape(n_chunks, 1, tg)
    return pl.pallas_call(
        functools.partial(_dispatch_kernel, tg=tg),
        grid_spec=pltpu.PrefetchScalarGridSpec(
            num_scalar_prefetch=1,
            grid=(n_chunks,),
            in_specs=[pl.BlockSpec((1, 1, tg), lambda c, cr: (c, 0, 0), memory_space=pltpu.SMEM),
                      pl.BlockSpec((1, 1, tg), lambda c, cr: (jnp.minimum(c + 1, n_chunks - 1), 0, 0),
                                   memory_space=pltpu.SMEM),
                      pl.BlockSpec(memory_space=pl.ANY),
                      pl.BlockSpec((1, d), lambda c, cr: (0, 0))],
            out_specs=pl.BlockSpec((tg, d), lambda c, cr: (c, 0)),
            scratch_shapes=[pltpu.VMEM((2, tg, d), F32), pltpu.SemaphoreType.DMA((2,))]),
        out_shape=jax.ShapeDtypeStruct((n_chunks * tg, d), BF16),
        compiler_params=_params("arbitrary"),
        name="moe_dispatch",
    )(chunk_rows, idx, idx, y, g.reshape(1, d))


def _expert_tile_state(te_ref, tr_ref, cnt_ref):
    i = pl.program_id(1)
    nu = cnt_ref[0]
    ic = jnp.minimum(i, nu - 1)
    new_weights = (i == 0) | (te_ref[ic] != te_ref[jnp.maximum(ic - 1, 0)])
    live = i < nu
    return live, live & new_weights, tr_ref[ic], ic


MOE_K_CHUNK = 512


def _for_live_rows(live, recast, n_valid, tm, o_ref, compute):
    half = tm // 2

    @pl.when(jnp.logical_not(live))
    def _():
        o_ref[...] = jnp.zeros(o_ref.shape, o_ref.dtype)

    @pl.when(recast)
    def _():
        compute(0, tm, True)

    @pl.when(live & jnp.logical_not(recast) & (n_valid > half))
    def _():
        compute(0, tm, False)

    @pl.when(live & jnp.logical_not(recast) & (n_valid <= half))
    def _():
        compute(0, half, False)
        o_ref[half:tm, :] = jnp.zeros((tm - half, o_ref.shape[1]), o_ref.dtype)


def _weight_copies(w_hbm, wbuf_ref, sem_ref, e, slot, col_blocks, tn):
    return [pltpu.make_async_copy(w_hbm.at[e, :, pl.ds(pl.multiple_of(cb * tn, tn), tn)],
                                  wbuf_ref.at[slot, m], sem_ref.at[slot])
            for m, cb in enumerate(col_blocks)]


def _stream_expert_weights(te_ref, ne_ref, gi_ref, cnt_ref, ic, recast, w_hbm, wbuf_ref, sem_ref, col_blocks, tn,
                           use_fresh):
    j = pl.program_id(0)
    i = pl.program_id(1)
    nj = pl.num_programs(0)
    slot = (j * cnt_ref[1] + gi_ref[ic]) % 2
    copies = functools.partial(_weight_copies, w_hbm, wbuf_ref, sem_ref)

    @pl.when((j == 0) & (i == 0))
    def _():
        for c in copies(te_ref[0], 0, col_blocks(0), tn):
            c.start()

    @pl.when(recast)
    def _():
        for c in copies(te_ref[ic], slot, col_blocks(j), tn):
            c.wait()
        nxt = ne_ref[ic]

        @pl.when(nxt >= 0)
        def _():
            for c in copies(nxt, 1 - slot, col_blocks(j), tn):
                c.start()

        @pl.when((nxt < 0) & (j + 1 < nj))
        def _():
            for c in copies(te_ref[0], 1 - slot, col_blocks(j + 1), tn):
                c.start()

        use_fresh(slot)


def _dot_casting_weights(x, wbuf_ref, slot, wb_refs):
    k = x.shape[1]
    kc = min(MOE_K_CHUNK, k)
    accs = [None] * len(wb_refs)
    for c0 in range(0, k, kc):
        xk = x[:, c0:c0 + kc]
        for m, wb_ref in enumerate(wb_refs):
            wb = wbuf_ref[slot, m, c0:c0 + kc, :].astype(BF16)
            wb_ref[c0:c0 + kc, :] = wb
            part = jnp.dot(xk, wb, preferred_element_type=F32)
            accs[m] = part if accs[m] is None else accs[m] + part
    return accs


def _moe_up_kernel(te_ref, tr_ref, ne_ref, gi_ref, cnt_ref, x_ref, w_hbm, bg_ref, bl_ref, o_ref,
                   wbuf_ref, sem_ref, wgb_ref, wlb_ref, *, tm, tn):
    live, recast, n_valid, ic = _expert_tile_state(te_ref, tr_ref, cnt_ref)
    nj = pl.num_programs(0)

    def finish(r0, r1, hg, hl):
        x_glu = jnp.minimum(hg + bg_ref[0], SWIGLU_LIMIT)
        x_lin = jnp.clip(hl + bl_ref[0], -SWIGLU_LIMIT, SWIGLU_LIMIT)
        o_ref[r0:r1, :] = (x_glu * jax.nn.sigmoid(SWIGLU_ALPHA * x_glu) * (x_lin + 1.0)).astype(o_ref.dtype)

    def use_fresh(slot):
        hg, hl = _dot_casting_weights(x_ref[...], wbuf_ref, slot, (wgb_ref, wlb_ref))
        finish(0, tm, hg, hl)

    def compute(r0, r1, fresh):
        if fresh:
            _stream_expert_weights(te_ref, ne_ref, gi_ref, cnt_ref, ic, recast, w_hbm, wbuf_ref, sem_ref,
                                   lambda jj: (jj, nj + jj), tn, use_fresh)
            return
        x = x_ref[r0:r1, :]
        finish(r0, r1, jnp.dot(x, wgb_ref[...], preferred_element_type=F32),
               jnp.dot(x, wlb_ref[...], preferred_element_type=F32))

    _for_live_rows(live, recast, n_valid, tm, o_ref, compute)


def _moe_down_kernel(te_ref, tr_ref, ne_ref, gi_ref, cnt_ref, a_ref, w_hbm, b_ref, g_ref, o_ref,
                     wbuf_ref, sem_ref, wb_ref, *, tm, tn):
    live, recast, n_valid, ic = _expert_tile_state(te_ref, tr_ref, cnt_ref)

    def finish(r0, r1, y):
        o_ref[r0:r1, :] = ((y + b_ref[0]) * g_ref[r0:r1, :]).astype(o_ref.dtype)

    def use_fresh(slot):
        (y,) = _dot_casting_weights(a_ref[...], wbuf_ref, slot, (wb_ref,))
        finish(0, tm, y)

    def compute(r0, r1, fresh):
        if fresh:
            _stream_expert_weights(te_ref, ne_ref, gi_ref, cnt_ref, ic, recast, w_hbm, wbuf_ref, sem_ref,
                                   lambda jj: (jj,), tn, use_fresh)
            return
        finish(r0, r1, jnp.dot(a_ref[r0:r1, :], wb_ref[...], preferred_element_type=F32))

    _for_live_rows(live, recast, n_valid, tm, o_ref, compute)


def moe_experts(xs, meta, row_g, w_up, b_up, w_down, b_down, *, tm, tn_up, tn_down):
    r, d = xs.shape
    ne, _, two_ff = w_up.shape
    d_ff = two_ff // 2
    n_tiles = r // tm
    row_i = lambda j, i, te, tr, nx, gi, cnt: (jnp.minimum(i, cnt[0] - 1), 0)
    exp_i = lambda i, te, cnt: te[jnp.minimum(i, cnt[0] - 1)]
    nj = d_ff // tn_up
    hbm = pl.BlockSpec(memory_space=pl.ANY)
    act = pl.pallas_call(
        functools.partial(_moe_up_kernel, tm=tm, tn=tn_up),
        grid_spec=pltpu.PrefetchScalarGridSpec(
            num_scalar_prefetch=5,
            grid=(nj, n_tiles),
            in_specs=[pl.BlockSpec((tm, d), row_i), hbm,
                      pl.BlockSpec((1, 1, tn_up), lambda j, i, te, tr, nx, gi, cnt: (exp_i(i, te, cnt), 0, j)),
                      pl.BlockSpec((1, 1, tn_up), lambda j, i, te, tr, nx, gi, cnt: (exp_i(i, te, cnt), 0, nj + j))],
            out_specs=pl.BlockSpec((tm, tn_up), lambda j, i, te, tr, nx, gi, cnt: (i, j)),
            scratch_shapes=[pltpu.VMEM((2, 2, d, tn_up), F32), pltpu.SemaphoreType.DMA((2,)),
                            pltpu.VMEM((d, tn_up), BF16), pltpu.VMEM((d, tn_up), BF16)]),
        out_shape=jax.ShapeDtypeStruct((r, d_ff), BF16),
        compiler_params=_params("arbitrary", "arbitrary"),
        name="moe_up",
    )(*meta, xs, w_up, b_up.reshape(ne, 1, two_ff), b_up.reshape(ne, 1, two_ff))
    d_out = w_down.shape[2]
    return pl.pallas_call(
        functools.partial(_moe_down_kernel, tm=tm, tn=tn_down),
        grid_spec=pltpu.PrefetchScalarGridSpec(
            num_scalar_prefetch=5,
            grid=(d_out // tn_down, n_tiles),
            in_specs=[pl.BlockSpec((tm, d_ff), row_i), hbm,
                      pl.BlockSpec((1, 1, tn_down), lambda j, i, te, tr, nx, gi, cnt: (exp_i(i, te, cnt), 0, j)),
                      pl.BlockSpec((tm, 1), row_i)],
            out_specs=pl.BlockSpec((tm, tn_down), lambda j, i, te, tr, nx, gi, cnt: (i, j)),
            scratch_shapes=[pltpu.VMEM((2, 1, d_ff, tn_down), F32), pltpu.SemaphoreType.DMA((2,)),
                            pltpu.VMEM((d_ff, tn_down), BF16)]),
        out_shape=jax.ShapeDtypeStruct((r, d_out), F32),
        compiler_params=_params("arbitrary", "arbitrary"),
        name="moe_down",
    )(*meta, act, w_down, b_down.reshape(ne, 1, d_out), row_g.reshape(r, 1))


def _combine_kernel(idx_ref, idx_next_ref, y_ref, yb_hbm, o_ref, buf_ref, sem_ref, *, tq):
    i = pl.program_id(0)
    n_steps = pl.num_programs(0)
    slot = i % 2
    n_rows = TOP_K * tq

    @pl.when(i == 0)
    def _():
        _row_gather_start(idx_ref, n_rows, yb_hbm, buf_ref, sem_ref, 0)

    @pl.when(i + 1 < n_steps)
    def _():
        _row_gather_start(idx_next_ref, n_rows, yb_hbm, buf_ref, sem_ref, 1 - slot)

    _row_gather_wait(n_rows, yb_hbm, buf_ref, sem_ref, slot)
    acc = y_ref[...]
    for s in range(TOP_K):
        acc = acc + buf_ref[slot, s * tq:(s + 1) * tq, :]
    o_ref[...] = acc


def moe_combine(y, yb, dest, *, row0, n_rows, tq):
    d = y.shape[1]
    n_steps = n_rows // tq
    b0 = row0 // tq
    idx = dest[row0:row0 + n_rows].reshape(n_steps, tq, TOP_K).transpose(0, 2, 1).reshape(n_steps, 1, TOP_K * tq)
    return pl.pallas_call(
        functools.partial(_combine_kernel, tq=tq),
        grid=(n_steps,),
        in_specs=[pl.BlockSpec((1, 1, TOP_K * tq), lambda i: (i, 0, 0), memory_space=pltpu.SMEM),
                  pl.BlockSpec((1, 1, TOP_K * tq), lambda i: (jnp.minimum(i + 1, n_steps - 1), 0, 0),
                               memory_space=pltpu.SMEM),
                  pl.BlockSpec((tq, d), lambda i: (b0 + i, 0)),
                  pl.BlockSpec(memory_space=pl.ANY)],
        out_specs=pl.BlockSpec((tq, d), lambda i: (i, 0)),
        out_shape=jax.ShapeDtypeStruct((n_rows, d), F32),
        scratch_shapes=[pltpu.VMEM((2, TOP_K * tq, d), F32), pltpu.SemaphoreType.DMA((2,))],
        compiler_params=_params("arbitrary"),
        name="moe_combine",
    )(idx, idx, y, yb)


def _route_metadata(top_idx, gates, tm, tg):
    t = top_idx.shape[0]
    tk = t * TOP_K
    flat_e = top_idx.reshape(-1)
    onehot = (flat_e[:, None] == jnp.arange(N_EXPERTS, dtype=I32)[None, :]).astype(I32)
    csum = jnp.cumsum(onehot, axis=0)
    rank = jnp.take_along_axis(csum, flat_e[:, None], axis=1)[:, 0] - 1
    counts = csum[-1]
    padded = (counts + tm - 1) // tm * tm
    pends = jnp.cumsum(padded)
    pstarts = pends - padded
    dest = pstarts[flat_e] + rank
    n_tiles = (tk + N_EXPERTS * (tm - 1) + tm - 1) // tm
    rows = n_tiles * tm
    flat_tok = jnp.arange(tk, dtype=I32) // TOP_K
    row_tok = jnp.zeros((rows,), I32).at[dest].set(flat_tok)
    row_g = jnp.zeros((rows,), F32).at[dest].set(gates.reshape(-1))
    tile_start = jnp.arange(n_tiles, dtype=I32) * tm
    tile_e = jnp.minimum(jnp.searchsorted(pends, tile_start, side='right'), N_EXPERTS - 1).astype(I32)
    tile_rows = jnp.clip(counts[tile_e] - (tile_start - pstarts[tile_e]), 0, tm).astype(I32)
    per = tm // tg
    chunk_rows = jnp.clip(jnp.repeat(tile_rows, per) - jnp.tile(jnp.arange(per, dtype=I32) * tg, n_tiles), 0, tg)
    has_rows = counts > 0
    ids = jnp.arange(N_EXPERTS, dtype=I32)
    later = jnp.where(has_rows[None, :] & (ids[None, :] > ids[:, None]), ids[None, :], N_EXPERTS)
    next_e = jnp.min(later, axis=1)
    next_e = jnp.where(next_e < N_EXPERTS, next_e, -1).astype(I32)
    group_i = (jnp.cumsum(has_rows.astype(I32)) - 1).astype(I32)
    cnt = jnp.stack([pends[-1] // tm, jnp.sum(has_rows.astype(I32))]).astype(I32)
    meta = (tile_e, tile_rows, next_e[tile_e], group_i[tile_e], cnt)
    return dest.reshape(t, TOP_K), row_tok, row_g, meta, chunk_rows.astype(I32)


TILES = dict(tm_light=320, tm=640, tn=512, tn_glu=256, tt_conv=256, tm_e=512, tg=256, tq=128, tn_up=512,
             tn_down=512)


def kernel(x_prompt, x_sample, cache_k, cache_v, state_conv, page_table, attn_norm_g, w_in, conv_w, conv_b,
           conv_ln_g, conv_ln_b, w_conv_out, q_norm_g, k_norm_g, w_attn_out, w_o, ffn_norm_g, w_router,
           b_router, w_up, b_up, w_down, b_down):
    return _step(TILES, x_prompt, x_sample, cache_k, cache_v, state_conv, page_table, attn_norm_g, w_in, conv_w,
                 conv_b, conv_ln_g, conv_ln_b, w_conv_out, q_norm_g, k_norm_g, w_attn_out, w_o, ffn_norm_g,
                 w_router, b_router, w_up, b_up, w_down, b_down)


def _step(cfg, x_prompt, x_sample, cache_k, cache_v, state_conv, page_table, attn_norm_g, w_in, conv_w, conv_b,
          conv_ln_g, conv_ln_b, w_conv_out, q_norm_g, k_norm_g, w_attn_out, w_o, ffn_norm_g, w_router,
          b_router, w_up, b_up, w_down, b_down):
    n_batch, seq_len, d_model = x_prompt.shape
    db, ds, _ = x_sample.shape
    d_conv = conv_w.shape[1]
    d_attn = w_attn_out.shape[0]
    n_heads = d_attn // HEAD_DIM
    past_len = page_table.shape[1] * PAGE_SIZE
    tp = n_batch * seq_len
    ts = db * ds

    x = jnp.concatenate([x_prompt.reshape(tp, d_model), x_sample.reshape(ts, d_model)], axis=0)
    n = rmsnorm_rows(x, attn_norm_g, cfg["tm_light"], BF16)

    proj = functools.partial(dense_proj, n, w_in, tm=cfg["tm"])
    u = proj(0, d_conv, kind="glu", out_dtype=F32, tn=cfg["tn_glu"])
    c = 2 * d_conv
    q = proj(c, d_attn, kind="headnorm", out_dtype=F32, tn=cfg["tn"], extra=q_norm_g)
    k = proj(c + d_attn, d_attn, kind="headnorm", out_dtype=F32, tn=cfg["tn"], extra=k_norm_g)
    v = proj(c + 2 * d_attn, d_attn, kind="none", out_dtype=F32, tn=cfg["tn"])
    sg = proj(c + 3 * d_attn, 2 * d_model, kind="sigmoid", out_dtype=BF16, tn=cfg["tn"])

    conv = functools.partial(conv_branch_act, conv_w=conv_w, conv_b=conv_b, ln_g=conv_ln_g, ln_b=conv_ln_b,
                             out_dtype=BF16)
    c_prompt = conv(u, None, n_seq=n_batch, seq_len=seq_len, tt=cfg["tt_conv"])
    u_s = u[tp:].reshape(db, ds, d_conv)
    ds_pad = -(-ds // 16) * 16
    lead = CONV_HALO - (CONV_WIDTH - 1)
    halo_s = jnp.pad(state_conv.astype(F32), ((0, 0), (lead, 0), (0, 0))).reshape(db * CONV_HALO, d_conv)
    u_s_pad = jnp.pad(u_s, ((0, 0), (0, ds_pad - ds), (0, 0))).reshape(db * ds_pad, d_conv)
    c_sample = conv(u_s_pad, halo_s, n_seq=db, seq_len=ds_pad, tt=ds_pad)
    c_sample = c_sample.reshape(db, ds_pad, d_conv)[:, :ds].reshape(ts, d_conv)
    c_act = jnp.concatenate([c_prompt[:tp], c_sample], axis=0)

    a_prompt = moba_prompt_attention(q, k, v, n_batch=n_batch, seq_len=seq_len, out_dtype=BF16)
    heads = lambda t_: t_[tp:].reshape(db, ds, n_heads, HEAD_DIM)
    a_sample = moba_sample_attention(heads(q), heads(k), heads(v), cache_k, cache_v, page_table)
    attn = jnp.concatenate([a_prompt, a_sample.astype(BF16)], axis=0)

    merged = gated_branch_merge(c_act, attn, sg, w_conv_out, w_attn_out, tm=cfg["tm"], tn=cfg["tn"], out_dtype=BF16)
    y1 = dense_proj(merged, w_o, 0, d_model, kind="residual", out_dtype=F32, tm=cfg["tm"], tn=cfg["tn"], extra=x)

    idx_pad, gate_pad = moe_router(y1, ffn_norm_g, w_router, b_router, tm=cfg["tm_light"])
    dest, row_tok, row_g, meta, chunk_rows = _route_metadata(
        idx_pad[:, :TOP_K], gate_pad[:, :TOP_K], cfg["tm_e"], cfg["tg"])
    xs = moe_dispatch(y1, ffn_norm_g, row_tok, chunk_rows, tg=cfg["tg"])
    yb = moe_experts(xs, meta, row_g, w_up, b_up, w_down, b_down,
                     tm=cfg["tm_e"], tn_up=cfg["tn_up"], tn_down=cfg["tn_down"])
    y_p = moe_combine(y1, yb, dest, row0=0, n_rows=tp, tq=cfg["tq"])
    y_s = moe_combine(y1, yb, dest, row0=tp, n_rows=ts, tq=cfg["tq"])

    u_p = u[:tp].reshape(n_batch, seq_len, d_conv)
    conv_prompt = jnp.pad(u_p, ((0, 0), (CONV_WIDTH - 1, 0), (0, 0)))[:, -(CONV_WIDTH - 1):]
    conv_sample = jnp.concatenate([state_conv.astype(F32), u_s], axis=1)[:, -(CONV_WIDTH - 1):]
    hp = lambda t_: t_[:tp].reshape(n_batch, seq_len, n_heads, HEAD_DIM)
    return (y_p.reshape(n_batch, seq_len, d_model), y_s.reshape(db, ds, d_model),
            hp(k), hp(v), conv_prompt, heads(k), heads(v), conv_sample)
```
